```python
import jax, jax.numpy as jnp
from jax import lax
import numpy as np

D_MODEL = 1024
BATCH = 1
SEQ = 16384
DEPTH = 2

GRID_W = 64
CTX_LEN = 256
N_Q_HEADS = 8
N_KV_HEADS = 2
HEAD_DIM = 128
ATTN_WIDTH = N_Q_HEADS * HEAD_DIM
KV_WIDTH = N_KV_HEADS * HEAD_DIM
ROPE_AXIS_DIM = HEAD_DIM // 2
ROPE_THETA = 10000.0
Q_BLOCK = 128
SGU_GROUPS = 8
SGU_WIDTH = 512
SGU_CHUNK = 128
FOURIER_GROUPS = 4
FOURIER_WIDTH = 512
N_BRANCHES = 3
FFN_HIDDEN = -(-8 * D_MODEL // (3 * 256)) * 256
Q_END = ATTN_WIDTH
K_END = Q_END + KV_WIDTH
V_END = K_END + KV_WIDTH
SGU_END = V_END + 2 * SGU_WIDTH
FOURIER_END = SGU_END + FOURIER_WIDTH
IN_WIDTH = FOURIER_END + N_BRANCHES * D_MODEL
DEEPNORM_ALPHA = (2 * DEPTH) ** 0.25
DEEPNORM_BETA = (8 * DEPTH) ** -0.25
LN_EPS = 1e-6
RMS_EPS = 1e-6

kernel_name = 'hybrid_gated_attn_gmlp_fourier_dit_block'


def layer_norm(x, g=None, b=None):
    xf = x.astype(jnp.float32)
    mu = jnp.mean(xf, axis=-1, keepdims=True)
    var = jnp.mean(jnp.square(xf - mu), axis=-1, keepdims=True)
    y = (xf - mu) * lax.rsqrt(var + LN_EPS)
    if g is not None:
        y = y * g.astype(jnp.float32) + b.astype(jnp.float32)
    return y.astype(x.dtype)


def rms_norm(x, g):
    xf = x.astype(jnp.float32)
    y = xf * lax.rsqrt(jnp.mean(jnp.square(xf), axis=-1, keepdims=True) + RMS_EPS)
    return (y * g.astype(jnp.float32)).astype(x.dtype)


def modulate(x, shift, scale):
    return layer_norm(x) * (1 + scale) + shift


def axial_rope(n_tokens):
    rows = n_tokens // GRID_W
    pos_r = jnp.repeat(jnp.arange(rows, dtype=jnp.float32), GRID_W)
    pos_c = jnp.tile(jnp.arange(GRID_W, dtype=jnp.float32), rows)
    inv = ROPE_THETA ** (-jnp.arange(0, ROPE_AXIS_DIM, 2, dtype=jnp.float32) / ROPE_AXIS_DIM)
    ang = jnp.concatenate([pos_r[:, None] * inv, pos_c[:, None] * inv], axis=-1)
    return jnp.cos(ang), jnp.sin(ang)


def apply_rope(x, cos, sin):
    xf = x.astype(jnp.float32).reshape(*x.shape[:-1], HEAD_DIM // 2, 2)
    x0, x1 = xf[..., 0], xf[..., 1]
    cc, ss = cos[None, :, None, :], sin[None, :, None, :]
    out = jnp.stack([x0 * cc - x1 * ss, x0 * ss + x1 * cc], axis=-1)
    return out.reshape(x.shape).astype(x.dtype)


def attention(q, k, v):
    B, Tq = q.shape[0], q.shape[1]
    G = N_Q_HEADS // N_KV_HEADS
    nb = Tq // Q_BLOCK
    qb = q.reshape(B, nb, Q_BLOCK, N_KV_HEADS, G, HEAD_DIM).transpose(1, 0, 2, 3, 4, 5)
    scale = HEAD_DIM ** -0.5

    def one_block(q_blk):
        s = jnp.einsum('bqhgd,bkhd->bhgqk', q_blk, k).astype(jnp.float32) * scale
        p = jax.nn.softmax(s, axis=-1).astype(v.dtype)
        return jnp.einsum('bhgqk,bkhd->bqhgd', p, v)

    o = lax.map(one_block, qb)
    return o.transpose(1, 0, 2, 3, 4, 5).reshape(B, Tq, ATTN_WIDTH)


def spatial_gating(s, ln_g, ln_b, w_sp, b_sp):
    B, T = s.shape[0], s.shape[1]
    u, v = s[..., :SGU_WIDTH], s[..., SGU_WIDTH:]
    v = layer_norm(v, ln_g, ln_b)
    n = T // SGU_CHUNK
    vg = v.reshape(B, n, SGU_CHUNK, SGU_GROUPS, SGU_WIDTH // SGU_GROUPS)
    mixed = jnp.einsum('gpq,bnqgc->bnpgc', w_sp, vg) + b_sp.T[None, None, :, :, None]
    return u * mixed.reshape(B, T, SGU_WIDTH)


def fourier_mix(f):
    B, T = f.shape[0], f.shape[1]
    fg = f.astype(jnp.float32).reshape(B, T, FOURIER_GROUPS, FOURIER_WIDTH // FOURIER_GROUPS)
    y = jnp.fft.fft2(fg, axes=(1, 3), norm='ortho').real
    return y.reshape(B, T, FOURIER_WIDTH).astype(f.dtype)


def token_mixer(h, w_in, q_gain, k_gain, sgu_ln_g, sgu_ln_b, w_sp, b_sp,
                w_br_attn, w_br_sgu, w_br_fourier, w_out, rope, ctx_kv):
    B, T, _ = h.shape
    z = h @ w_in
    q = rms_norm(z[..., :Q_END].reshape(B, T, N_Q_HEADS, HEAD_DIM), q_gain)
    k = rms_norm(z[..., Q_END:K_END].reshape(B, T, N_KV_HEADS, HEAD_DIM), k_gain)
    v = z[..., K_END:V_END].reshape(B, T, N_KV_HEADS, HEAD_DIM)
    s = jax.nn.gelu(z[..., V_END:SGU_END])
    f = z[..., SGU_END:FOURIER_END]
    gates = jax.nn.sigmoid(z[..., FOURIER_END:].astype(jnp.float32)).astype(h.dtype)
    gates = gates.reshape(B, T, N_BRANCHES, D_MODEL)
    if rope is not None:
        q = apply_rope(q, rope[0], rope[1])
        k = apply_rope(k, rope[0], rope[1])
    if ctx_kv is not None:
        k_all = jnp.concatenate([k, ctx_kv[0]], axis=1)
        v_all = jnp.concatenate([v, ctx_kv[1]], axis=1)
    else:
        k_all, v_all = k, v
    a = attention(q, k_all, v_all)
    g_out = spatial_gating(s, sgu_ln_g, sgu_ln_b, w_sp, b_sp)
    f_out = fourier_mix(f)
    merged = (gates[:, :, 0] * (a @ w_br_attn)
              + gates[:, :, 1] * (g_out @ w_br_sgu)
              + gates[:, :, 2] * (f_out @ w_br_fourier))
    return merged @ w_out, (k, v)


def context_kv(h, w_in, k_gain):
    B, T, _ = h.shape
    kv = h @ w_in[:, Q_END:V_END]
    k = rms_norm(kv[..., :KV_WIDTH].reshape(B, T, N_KV_HEADS, HEAD_DIM), k_gain)
    v = kv[..., KV_WIDTH:].reshape(B, T, N_KV_HEADS, HEAD_DIM)
    return k, v


def swiglu(h, w_up, w_down):
    gu = h @ w_up
    return (jax.nn.silu(gu[..., :FFN_HIDDEN]) * gu[..., FFN_HIDDEN:]) @ w_down


def setup_inputs(seed: int = 0) -> dict:
    key = jax.random.key(seed)
    ks = jax.random.split(key, 24)
    f32 = jnp.float32
    L, D = DEPTH, D_MODEL

    def nrm(k, shape, scale):
        return jax.random.normal(k, shape, f32) * scale

    return {
        'x': nrm(ks[0], (BATCH, SEQ, D), 1.0),
        'c': nrm(ks[1], (BATCH, D), 1.0),
        'ctx': nrm(ks[2], (BATCH, CTX_LEN, D), 1.0),
        'c_ctx': nrm(ks[3], (D,), 1.0),
        'w_ada': nrm(ks[4], (L, D, 6 * D), D ** -0.5),
        'b_ada': nrm(ks[5], (L, 6 * D), 0.01),
        'w_in': nrm(ks[6], (L, D, IN_WIDTH), D ** -0.5),
        'q_gain': 1.0 + nrm(ks[7], (L, HEAD_DIM), 0.02),
        'k_gain': 1.0 + nrm(ks[8], (L, HEAD_DIM), 0.02),
        'sgu_ln_g': 1.0 + nrm(ks[9], (L, SGU_WIDTH), 0.02),
        'sgu_ln_b': nrm(ks[10], (L, SGU_WIDTH), 0.02),
        'w_spatial': nrm(ks[11], (L, SGU_GROUPS, SGU_CHUNK, SGU_CHUNK), SGU_CHUNK ** -0.5),
        'b_spatial': 1.0 + nrm(ks[12], (L, SGU_GROUPS, SGU_CHUNK), 0.02),
        'w_br_attn': nrm(ks[13], (L, ATTN_WIDTH, D), ATTN_WIDTH ** -0.5),
        'w_br_sgu': nrm(ks[14], (L, SGU_WIDTH, D), SGU_WIDTH ** -0.5),
        'w_br_fourier': nrm(ks[15], (L, FOURIER_WIDTH, D), FOURIER_WIDTH ** -0.5),
        'w_out': nrm(ks[16], (L, D, D), D ** -0.5 * DEEPNORM_BETA),
        'ln1_g': 1.0 + nrm(ks[17], (L, D), 0.02),
        'ln1_b': nrm(ks[18], (L, D), 0.02),
        'w_up': nrm(ks[19], (L, D, 2 * FFN_HIDDEN), D ** -0.5),
        'w_down': nrm(ks[20], (L, FFN_HIDDEN, D), FFN_HIDDEN ** -0.5 * DEEPNORM_BETA),
        'ln2_g': 1.0 + nrm(ks[21], (L, D), 0.02),
        'ln2_b': nrm(ks[22], (L, D), 0.02),
    }


def reference(x, c, ctx, c_ctx, w_ada, b_ada, w_in, q_gain, k_gain, sgu_ln_g, sgu_ln_b,
              w_spatial, b_spatial, w_br_attn, w_br_sgu, w_br_fourier, w_out,
              ln1_g, ln1_b, w_up, w_down, ln2_g, ln2_b):
    rope = axial_rope(x.shape[1])
    alpha = DEEPNORM_ALPHA
    for l in range(DEPTH):
        last = l == DEPTH - 1
        mod_x = (jax.nn.silu(c) @ w_ada[l] + b_ada[l])[:, None, :]
        mod_c = (jax.nn.silu(c_ctx) @ w_ada[l] + b_ada[l])[None, None, :]
        shift1, scale1, gate1, shift2, scale2, gate2 = jnp.split(mod_x, 6, axis=-1)
        c_shift1, c_scale1, c_gate1, c_shift2, c_scale2, c_gate2 = jnp.split(mod_c, 6, axis=-1)
        mixer_w = (w_in[l], q_gain[l], k_gain[l], sgu_ln_g[l], sgu_ln_b[l], w_spatial[l],
                   b_spatial[l], w_br_attn[l], w_br_sgu[l], w_br_fourier[l], w_out[l])

        hc = modulate(ctx, c_shift1, c_scale1)
        if last:
            ctx_kv = context_kv(hc, w_in[l], k_gain[l])
        else:
            mix_c, ctx_kv = token_mixer(hc, *mixer_w, None, None)

        hx = modulate(x, shift1, scale1)
        mix_x, _ = token_mixer(hx, *mixer_w, rope, ctx_kv)
        x = layer_norm(alpha * x + gate1 * mix_x, ln1_g[l], ln1_b[l])
        ffn_x = swiglu(modulate(x, shift2, scale2), w_up[l], w_down[l])
        x = layer_norm(alpha * x + gate2 * ffn_x, ln2_g[l], ln2_b[l])

        if not last:
            ctx = layer_norm(alpha * ctx + c_gate1 * mix_c, ln1_g[l], ln1_b[l])
            ffn_c = swiglu(modulate(ctx, c_shift2, c_scale2), w_up[l], w_down[l])
            ctx = layer_norm(alpha * ctx + c_gate2 * ffn_c, ln2_g[l], ln2_b[l])
    return x
```

```python
import functools
import math

import jax
import jax.numpy as jnp
import numpy as np
from jax import lax
from jax.experimental import pallas as pl
from jax.experimental.pallas import tpu as pltpu

F32 = jnp.float32
BF16 = jnp.bfloat16

D_MODEL = 1024
DEPTH = 2
GRID_W = 64
N_Q_HEADS = 8
N_KV_HEADS = 2
HEAD_DIM = 128
Q_PER_KV = N_Q_HEADS // N_KV_HEADS
ATTN_WIDTH = N_Q_HEADS * HEAD_DIM
KV_WIDTH = N_KV_HEADS * HEAD_DIM
ROPE_AXIS_DIM = HEAD_DIM // 2
ROPE_THETA = 10000.0
SGU_GROUPS = 8
SGU_WIDTH = 512
SGU_CHUNK = 128
FOURIER_GROUPS = 4
FOURIER_WIDTH = 512
FOURIER_GROUP_CH = FOURIER_WIDTH // FOURIER_GROUPS
N_BRANCHES = 3
FFN_HIDDEN = -(-8 * D_MODEL // (3 * 256)) * 256
IN_WIDTH = ATTN_WIDTH + 2 * KV_WIDTH + 2 * SGU_WIDTH + FOURIER_WIDTH + N_BRANCHES * D_MODEL
DEEPNORM_ALPHA = (2 * DEPTH) ** 0.25
LN_EPS = 1e-6
RMS_EPS = 1e-6
ATTN_SCALE = HEAD_DIM ** -0.5
Q_PRESCALE = ATTN_SCALE * math.log2(math.e)

COL_BLOCK = 512
N_COL_BLOCKS = IN_WIDTH // COL_BLOCK
CB_Q0 = 0
CB_KV = 2
CB_SGU_U = 3
CB_SGU_V = 4
CB_FOURIER = 5
CB_GATES = 6

LANES = 128
VMEM_LIMIT_BYTES = 52 * 1024 * 1024

ADA_TN = 1024
ATTN_BK = 512
ATTN_ROW_CHUNK = 256
FFN_CHUNKS = 2


def _cparams(sem):
    return pltpu.CompilerParams(dimension_semantics=sem, vmem_limit_bytes=VMEM_LIMIT_BYTES)


def _row_tile(t, want):
    tm = min(t, want)
    assert t % tm == 0
    return tm


def _sigmoid(x):
    return 1.0 / (1.0 + jnp.exp(-x))


def _gelu_tanh(x):
    c = math.sqrt(2.0 / math.pi)
    return 0.5 * x * (1.0 + jnp.tanh(c * (x + 0.044715 * (x * x * x))))


def _ln(x):
    mu = jnp.mean(x, axis=-1, keepdims=True)
    xc = x - mu
    var = jnp.mean(xc * xc, axis=-1, keepdims=True)
    return xc * lax.rsqrt(var + LN_EPS)


def _ada_kernel(c_ref, w_ref, b_ref, o_ref):
    c = c_ref[...]
    s = c * _sigmoid(c)
    w = w_ref[0]
    b = b_ref[0]
    r0 = jnp.sum(w * s[:, 0:1], axis=0, keepdims=True) + b
    r1 = jnp.sum(w * s[:, 1:2], axis=0, keepdims=True) + b
    o_ref[0] = jnp.concatenate([r0, r1], axis=0)


def _ada(c2, w_ada, b_ada):
    depth, d, n = w_ada.shape
    tn = ADA_TN
    return pl.pallas_call(
        _ada_kernel,
        grid=(depth, n // tn),
        in_specs=[
            pl.BlockSpec((d, 2), lambda l, j: (0, 0)),
            pl.BlockSpec((1, d, tn), lambda l, j: (l, 0, j)),
            pl.BlockSpec((1, 1, tn), lambda l, j: (l, 0, j)),
        ],
        out_specs=pl.BlockSpec((1, 2, tn), lambda l, j: (l, 0, j)),
        out_shape=jax.ShapeDtypeStruct((depth, 2, n), F32),
        compiler_params=_cparams(("arbitrary", "arbitrary")),
        name="ada",
    )(c2, w_ada, b_ada.reshape(depth, 1, n))


def _rms_rope(zh, gain, cos, sin_signed, use_rope):
    y = zh * lax.rsqrt(jnp.mean(zh * zh, axis=-1, keepdims=True) + RMS_EPS) * gain
    if not use_rope:
        return y
    lane = lax.broadcasted_iota(jnp.int32, y.shape, 1)
    nxt = pltpu.roll(y, HEAD_DIM - 1, axis=1)
    prv = pltpu.roll(y, 1, axis=1)
    partner = jnp.where((lane & 1) == 0, nxt, prv)
    return y * cos + partner * sin_signed


def _in_kernel(x_ref, sh_ref, sc_ref, w_ref, qg_ref, kg_ref, lg_ref, lb_ref, cos_ref, sin_ref,
               o_ref, h_ref, *, cb0, use_rope):
    j = pl.program_id(1)
    cb = j + cb0

    @pl.when(j == 0)
    def _():
        h = _ln(x_ref[...]) * (1.0 + sc_ref[...]) + sh_ref[...]
        h_ref[...] = h.astype(BF16)

    z = jnp.dot(h_ref[...], w_ref[...], preferred_element_type=F32)
    heads = COL_BLOCK // HEAD_DIM

    def qk_heads(first, count, gain):
        cos = cos_ref[...] if use_rope else None
        sin = sin_ref[...] if use_rope else None
        for hh in range(first, first + count):
            sl = slice(hh * HEAD_DIM, (hh + 1) * HEAD_DIM)
            o_ref[:, sl] = _rms_rope(z[:, sl], gain, cos, sin, use_rope).astype(BF16)

    @pl.when(cb < CB_KV)
    def _():
        qk_heads(0, heads, qg_ref[...] * Q_PRESCALE)

    @pl.when(cb == CB_KV)
    def _():
        qk_heads(0, N_KV_HEADS, kg_ref[...])
        o_ref[:, KV_WIDTH:] = z[:, KV_WIDTH:].astype(BF16)

    @pl.when(cb == CB_SGU_U)
    def _():
        o_ref[...] = _gelu_tanh(z).astype(BF16)

    @pl.when(cb == CB_SGU_V)
    def _():
        o_ref[...] = (_ln(_gelu_tanh(z)) * lg_ref[...] + lb_ref[...]).astype(BF16)

    @pl.when(cb == CB_FOURIER)
    def _():
        o_ref[...] = z.astype(BF16)

    @pl.when(cb >= CB_GATES)
    def _():
        o_ref[...] = _sigmoid(z).astype(BF16)


def _in_proj(x, shift, scale, w_in, q_gain, k_gain, sgu_g, sgu_b, rope, *, cb0, ncb, tm):
    t, d = x.shape
    use_rope = rope is not None
    if use_rope:
        cos, sin = rope
    else:
        cos = sin = jnp.zeros((8, HEAD_DIM), F32)
    rope_spec = (pl.BlockSpec((tm, HEAD_DIM), lambda i, j: (i, 0)) if use_rope
                 else pl.BlockSpec((8, HEAD_DIM), lambda i, j: (0, 0)))
    vec = lambda n: pl.BlockSpec((1, n), lambda i, j: (0, 0))
    return pl.pallas_call(
        functools.partial(_in_kernel, cb0=cb0, use_rope=use_rope),
        grid=(t // tm, ncb),
        in_specs=[
            pl.BlockSpec((tm, d), lambda i, j: (i, 0)),
            vec(d), vec(d),
            pl.BlockSpec((d, COL_BLOCK), lambda i, j: (0, j + cb0)),
            vec(HEAD_DIM), vec(HEAD_DIM), vec(SGU_WIDTH), vec(SGU_WIDTH),
            rope_spec, rope_spec,
        ],
        out_specs=pl.BlockSpec((tm, COL_BLOCK), lambda i, j: (i, j)),
        out_shape=jax.ShapeDtypeStruct((t, ncb * COL_BLOCK), BF16),
        scratch_shapes=[pltpu.VMEM((tm, d), BF16)],
        compiler_params=_cparams(("arbitrary", "arbitrary")),
        name="in_proj",
    )(x, shift, scale, w_in, q_gain, k_gain, sgu_g, sgu_b, cos, sin)


def _attn_kernel(*refs, tq, bk, n_main, bkc):
    if n_main:
        q_ref, kc_ref, vc_ref, k_ref, v_ref, o_ref = refs[:6]
        qs_ref, s0_ref, s1_ref, p0_ref, p1_ref, m_ref, l_ref, acc_ref = refs[6:]
    else:
        q_ref, kc_ref, vc_ref, o_ref = refs[:4]
        qs_ref, s1_ref, p1_ref, m_ref, l_ref, acc_ref = refs[4:]
    m_rows = Q_PER_KV * tq
    rc = min(ATTN_ROW_CHUNK, m_rows)

    for g in range(Q_PER_KV):
        qs_ref[g * tq:(g + 1) * tq, :] = q_ref[:, g * HEAD_DIM:(g + 1) * HEAD_DIM]

    def scores(kblk):
        return lax.dot_general(qs_ref[...], kblk, (((1,), (1,)), ((), ())),
                               preferred_element_type=F32)

    def softmax_pv(s_ref, p_ref, vblk, ncols, first):
        for r0 in range(0, m_rows, rc):
            rows = slice(r0, r0 + rc)
            cols = [s_ref[rows, c * LANES:(c + 1) * LANES] for c in range(ncols)]
            mx = cols[0]
            for c in cols[1:]:
                mx = jnp.maximum(mx, c)
            m_cur = jnp.max(mx, axis=-1, keepdims=True)
            if first:
                m_new = jnp.broadcast_to(m_cur, (rc, LANES))
            else:
                m_prev = m_ref[rows, :]
                m_new = jnp.maximum(m_prev, m_cur)
            ps = [jnp.exp2(c - m_new) for c in cols]
            psum = ps[0]
            for p in ps[1:]:
                psum = psum + p
            for c in range(ncols):
                p_ref[rows, c * LANES:(c + 1) * LANES] = ps[c].astype(BF16)
            if first:
                l_ref[rows, :] = psum
            else:
                alpha = jnp.exp2(m_prev - m_new)
                l_ref[rows, :] = alpha * l_ref[rows, :] + psum
                acc_ref[rows, :] = alpha * acc_ref[rows, :]
            m_ref[rows, :] = m_new
        pv = jnp.dot(p_ref[:, :ncols * LANES], vblk, preferred_element_type=F32)
        if first:
            acc_ref[...] = pv
        else:
            acc_ref[...] += pv

    def kblock(ref, i):
        return ref[pl.ds(pl.multiple_of(i * bk, bk), bk), :]

    if n_main:
        s0_ref[...] = scores(k_ref[0:bk, :])
    s1_ref[:, :bkc] = scores(kc_ref[...])
    softmax_pv(s1_ref, p1_ref, vc_ref[...], bkc // LANES, True)

    if n_main:
        ncols = bk // LANES

        def pair(j, produce_second):
            s1_ref[...] = scores(kblock(k_ref, 2 * j + 1))
            softmax_pv(s0_ref, p0_ref, kblock(v_ref, 2 * j), ncols, False)
            if produce_second:
                s0_ref[...] = scores(kblock(k_ref, 2 * j + 2))
            softmax_pv(s1_ref, p1_ref, kblock(v_ref, 2 * j + 1), ncols, False)

        def body(j, carry):
            pair(j, True)
            return carry

        lax.fori_loop(0, n_main // 2 - 1, body, 0)
        pair(n_main // 2 - 1, False)

    l_row = jnp.sum(l_ref[...], axis=-1, keepdims=True)
    o = acc_ref[...] / l_row
    for g in range(Q_PER_KV):
        o_ref[:, g * HEAD_DIM:(g + 1) * HEAD_DIM] = o[g * tq:(g + 1) * tq, :].astype(BF16)


def _attention(zq, ctx_kv, main_kv, *, tq):
    t = zq.shape[0]
    m_rows = Q_PER_KV * tq
    kv_spec = lambda length, col: pl.BlockSpec((length, HEAD_DIM), lambda h, i: (0, col + h))
    zc, kcol_c, vcol_c = ctx_kv
    bkc = zc.shape[0]
    assert bkc % LANES == 0
    args = [zq, zc, zc]
    in_specs = [pl.BlockSpec((tq, Q_PER_KV * HEAD_DIM), lambda h, i: (i, h)),
                kv_spec(bkc, kcol_c), kv_spec(bkc, vcol_c)]
    scratch = [pltpu.VMEM((m_rows, HEAD_DIM), BF16)]
    if main_kv is not None:
        zm, kcol_m, vcol_m = main_kv
        bk = ATTN_BK
        n_main = zm.shape[0] // bk
        assert n_main * bk == zm.shape[0] and n_main % 2 == 0 and bkc <= bk
        args += [zm, zm]
        in_specs += [kv_spec(zm.shape[0], kcol_m), kv_spec(zm.shape[0], vcol_m)]
        scratch += [pltpu.VMEM((m_rows, bk), F32), pltpu.VMEM((m_rows, bk), F32),
                    pltpu.VMEM((m_rows, bk), BF16), pltpu.VMEM((m_rows, bk), BF16)]
    else:
        bk, n_main = bkc, 0
        scratch += [pltpu.VMEM((m_rows, bkc), F32), pltpu.VMEM((m_rows, bkc), BF16)]
    scratch += [pltpu.VMEM((m_rows, LANES), F32), pltpu.VMEM((m_rows, LANES), F32),
                pltpu.VMEM((m_rows, HEAD_DIM), F32)]
    return pl.pallas_call(
        functools.partial(_attn_kernel, tq=tq, bk=bk, n_main=n_main, bkc=bkc),
        grid=(N_KV_HEADS, t // tq),
        in_specs=in_specs,
        out_specs=pl.BlockSpec((tq, Q_PER_KV * HEAD_DIM), lambda h, i: (i, h)),
        out_shape=jax.ShapeDtypeStruct((t, ATTN_WIDTH), BF16),
        scratch_shapes=scratch,
        compiler_params=_cparams(("arbitrary", "arbitrary")),
        name="attn",
    )(*args)


def _dft_tables(t):
    n2 = min(t, 128) if t > 256 else t
    n1 = t // n2
    assert n1 * n2 == t

    def cs(n):
        idx = np.arange(n)
        ang = 2.0 * np.pi * ((idx[:, None] * idx[None, :]) % n) / n
        return np.cos(ang), np.sin(ang)

    tabs = {"n1": n1, "n2": n2}
    c2, s2 = cs(n2)
    if n1 > 1:
        c1, s1 = cs(n1)
        tabs["m1"] = jnp.asarray(np.concatenate([c1, -s1], axis=0), F32)
        ang = 2.0 * np.pi * (np.arange(n1)[:, None] * np.arange(n2)[None, :]) / t
        tabs["twr"] = jnp.asarray(np.repeat(np.cos(ang), LANES, axis=1), F32)
        tabs["twi"] = jnp.asarray(np.repeat(-np.sin(ang), LANES, axis=1), F32)
        m2 = np.block([[c2, s2], [-s2, c2]])
    else:
        m2 = np.concatenate([c2, -s2], axis=0)
    tabs["m2"] = jnp.asarray(m2, F32)
    cc, sc = cs(FOURIER_GROUP_CH)
    tabs["mc"] = jnp.asarray(np.concatenate([cc, sc], axis=0), F32)
    tabs["norm"] = 1.0 / math.sqrt(t * FOURIER_GROUP_CH)
    return tabs


def _dft_a_kernel(f_ref, m1_ref, twr_ref, twi_ref, xr_ref, xi_ref, *, n1, nt2):
    w = FOURIER_WIDTH
    for j in range(nt2):
        x1 = jnp.dot(m1_ref[...], f_ref[:, j * w:(j + 1) * w], preferred_element_type=F32)
        wr = twr_ref[:, j * LANES:(j + 1) * LANES]
        wi = twi_ref[:, j * LANES:(j + 1) * LANES]
        for g in range(w // LANES):
            a = x1[:n1, g * LANES:(g + 1) * LANES]
            b = x1[n1:, g * LANES:(g + 1) * LANES]
            sl = slice(j * w + g * LANES, j * w + (g + 1) * LANES)
            xr_ref[:, sl] = (a * wr - b * wi).astype(BF16)
            xi_ref[:, sl] = (a * wi + b * wr).astype(BF16)


def _dft_b_kernel(*refs, n2, kb, complex_in, norm):
    if complex_in:
        xr_ref, xi_ref, m2_ref, mc_ref, o_ref = refs
    else:
        xr_ref, m2_ref, mc_ref, o_ref = refs
    c = FOURIER_GROUP_CH
    for j in range(kb):
        if complex_in:
            rhs = jnp.concatenate([xr_ref[j], xi_ref[j]], axis=0)
        else:
            rhs = xr_ref[j]
        p = jnp.dot(m2_ref[...], rhs, preferred_element_type=F32)
        for g in range(FOURIER_GROUPS):
            pg = jnp.concatenate([p[:n2, g * c:(g + 1) * c], p[n2:, g * c:(g + 1) * c]], axis=1)
            y = jnp.dot(pg.astype(BF16), mc_ref[...], preferred_element_type=F32) * norm
            o_ref[:, j, g * c:(g + 1) * c] = y.astype(BF16)


def _fourier(z, tabs):
    t = z.shape[0]
    n1, n2, w = tabs["n1"], tabs["n2"], FOURIER_WIDTH
    if n1 > 1:
        f = z[:, CB_FOURIER * COL_BLOCK:(CB_FOURIER + 1) * COL_BLOCK].reshape(n1, n2 * w)
        nt2 = 8
        assert n2 % nt2 == 0
        xr, xi = pl.pallas_call(
            functools.partial(_dft_a_kernel, n1=n1, nt2=nt2),
            grid=(n2 // nt2,),
            in_specs=[
                pl.BlockSpec((n1, nt2 * w), lambda i: (0, i)),
                pl.BlockSpec((2 * n1, n1), lambda i: (0, 0)),
                pl.BlockSpec((n1, nt2 * LANES), lambda i: (0, i)),
                pl.BlockSpec((n1, nt2 * LANES), lambda i: (0, i)),
            ],
            out_specs=[pl.BlockSpec((n1, nt2 * w), lambda i: (0, i))] * 2,
            out_shape=[jax.ShapeDtypeStruct((n1, n2 * w), BF16)] * 2,
            compiler_params=_cparams(("arbitrary",)),
            name="dft_a",
        )(f, tabs["m1"].astype(BF16), tabs["twr"], tabs["twi"])
        kb = 8
        assert n1 % kb == 0
        xs = [xr.reshape(n1, n2, w), xi.reshape(n1, n2, w)]
        x_specs = [pl.BlockSpec((kb, n2, w), lambda i: (i, 0, 0))] * 2
        complex_in = True
    else:
        kb = 1
        xs = [z.reshape(1, t, z.shape[1])]
        x_specs = [pl.BlockSpec((1, n2, w), lambda i: (0, 0, CB_FOURIER))]
        complex_in = False
    m2, mc = tabs["m2"].astype(BF16), tabs["mc"].astype(BF16)
    out = pl.pallas_call(
        functools.partial(_dft_b_kernel, n2=n2, kb=kb, complex_in=complex_in, norm=tabs["norm"]),
        grid=(n1 // kb,),
        in_specs=x_specs + [
            pl.BlockSpec(m2.shape, lambda i: (0, 0)),
            pl.BlockSpec(mc.shape, lambda i: (0, 0)),
        ],
        out_specs=pl.BlockSpec((n2, kb, w), lambda i: (0, i, 0)),
        out_shape=jax.ShapeDtypeStruct((n2, n1, w), BF16),
        compiler_params=_cparams(("arbitrary",)),
        name="dft_b",
    )(*xs, m2, mc)
    return out.reshape(t, w)


def _merge_kernel(a_ref, u_ref, vn_ref, fo_ref, g_ref, x_ref, wa_ref, ws_ref, wf_ref, wo_ref,
                  wsp_ref, bsp_ref, gate_ref, lg_ref, lb_ref, o_ref, gout_ref, *, tm):
    group_ch = SGU_WIDTH // SGU_GROUPS
    lane = lax.broadcasted_iota(jnp.int32, (SGU_CHUNK, LANES), 1)
    for ci in range(tm // SGU_CHUNK):
        rows = slice(ci * SGU_CHUNK, (ci + 1) * SGU_CHUNK)
        for lb in range(SGU_WIDTH // LANES):
            cols = slice(lb * LANES, (lb + 1) * LANES)
            y = jnp.dot(wsp_ref[lb], vn_ref[rows, cols], preferred_element_type=F32)
            mixed = jnp.where(lane < group_ch, y[:SGU_CHUNK], y[SGU_CHUNK:]) + bsp_ref[:, cols]
            gout_ref[rows, cols] = (u_ref[rows, cols].astype(F32) * mixed).astype(BF16)

    d = D_MODEL
    merged = g_ref[:, 0:d].astype(F32) * jnp.dot(a_ref[...], wa_ref[...], preferred_element_type=F32)
    merged += g_ref[:, d:2 * d].astype(F32) * jnp.dot(gout_ref[...], ws_ref[...],
                                                      preferred_element_type=F32)
    merged += g_ref[:, 2 * d:3 * d].astype(F32) * jnp.dot(fo_ref[...], wf_ref[...],
                                                          preferred_element_type=F32)
    mix = jnp.dot(merged.astype(BF16), wo_ref[...], preferred_element_type=F32)
    y = DEEPNORM_ALPHA * x_ref[...] + gate_ref[...] * mix
    o_ref[...] = _ln(y) * lg_ref[...] + lb_ref[...]


def _merge(a, z, f_out, x, w_br_attn, w_br_sgu, w_br_fourier, w_out, wsp2, bsp, gate1, ln_g, ln_b, *, tm):
    t, d = x.shape
    full = lambda arr: pl.BlockSpec(arr.shape, lambda i: (0,) * arr.ndim)
    vec = pl.BlockSpec((1, d), lambda i: (0, 0))
    return pl.pallas_call(
        functools.partial(_merge_kernel, tm=tm),
        grid=(t // tm,),
        in_specs=[
            pl.BlockSpec((tm, ATTN_WIDTH), lambda i: (i, 0)),
            pl.BlockSpec((tm, COL_BLOCK), lambda i: (i, CB_SGU_U)),
            pl.BlockSpec((tm, COL_BLOCK), lambda i: (i, CB_SGU_V)),
            pl.BlockSpec((tm, FOURIER_WIDTH), lambda i: (i, 0)),
            pl.BlockSpec((tm, N_BRANCHES * d), lambda i: (i, CB_GATES * COL_BLOCK // (N_BRANCHES * d))),
            pl.BlockSpec((tm, d), lambda i: (i, 0)),
            full(w_br_attn), full(w_br_sgu), full(w_br_fourier), full(w_out),
            full(wsp2), full(bsp), vec, vec, vec,
        ],
        out_specs=pl.BlockSpec((tm, d), lambda i: (i, 0)),
        out_shape=jax.ShapeDtypeStruct((t, d), F32),
        scratch_shapes=[pltpu.VMEM((tm, SGU_WIDTH), BF16)],
        compiler_params=_cparams(("arbitrary",)),
        name="merge",
    )(a, z, z, f_out, z, x, w_br_attn, w_br_sgu, w_br_fourier, w_out, wsp2, bsp, gate1, ln_g, ln_b)


def _ffn_kernel(x_ref, sh_ref, sc_ref, gate_ref, wg_ref, wu_ref, wd_ref, lg_ref, lb_ref,
                o_ref, h_ref, acc_ref):
    c = pl.program_id(1)

    @pl.when(c == 0)
    def _():
        h = _ln(x_ref[...]) * (1.0 + sc_ref[...]) + sh_ref[...]
        h_ref[...] = h.astype(BF16)
        acc_ref[...] = jnp.zeros(acc_ref.shape, F32)

    h = h_ref[...]
    g = jnp.dot(h, wg_ref[...], preferred_element_type=F32)
    u = jnp.dot(h, wu_ref[...], preferred_element_type=F32)
    act = (g * _sigmoid(g) * u).astype(BF16)
    acc_ref[...] += jnp.dot(act, wd_ref[...], preferred_element_type=F32)

    @pl.when(c == pl.num_programs(1) - 1)
    def _():
        y = DEEPNORM_ALPHA * x_ref[...] + gate_ref[...] * acc_ref[...]
        o_ref[...] = _ln(y) * lg_ref[...] + lb_ref[...]


def _ffn(x, shift, scale, gate, w_up, w_down, ln_g, ln_b, *, tm):
    t, d = x.shape
    hidden = w_down.shape[0]
    th = hidden // FFN_CHUNKS
    assert th * FFN_CHUNKS == hidden and th % LANES == 0
    vec = pl.BlockSpec((1, d), lambda i, c: (0, 0))
    return pl.pallas_call(
        _ffn_kernel,
        grid=(t // tm, FFN_CHUNKS),
        in_specs=[
            pl.BlockSpec((tm, d), lambda i, c: (i, 0)),
            vec, vec, vec,
            pl.BlockSpec((d, th), lambda i, c: (0, c)),
            pl.BlockSpec((d, th), lambda i, c: (0, c + FFN_CHUNKS)),
            pl.BlockSpec((th, d), lambda i, c: (c, 0)),
            vec, vec,
        ],
        out_specs=pl.BlockSpec((tm, d), lambda i, c: (i, 0)),
        out_shape=jax.ShapeDtypeStruct((t, d), F32),
        scratch_shapes=[pltpu.VMEM((tm, d), BF16), pltpu.VMEM((tm, d), F32)],
        compiler_params=_cparams(("arbitrary", "arbitrary")),
        name="ffn",
    )(x, shift, scale, gate, w_up, w_up, w_down, ln_g, ln_b)


def _rope_tables(n_tokens):
    rows = n_tokens // GRID_W
    pos_r = jnp.repeat(jnp.arange(rows, dtype=F32), GRID_W)
    pos_c = jnp.tile(jnp.arange(GRID_W, dtype=F32), rows)
    inv = ROPE_THETA ** (-jnp.arange(0, ROPE_AXIS_DIM, 2, dtype=F32) / ROPE_AXIS_DIM)
    ang = jnp.concatenate([pos_r[:, None] * inv, pos_c[:, None] * inv], axis=-1)
    cos, sin = jnp.cos(ang), jnp.sin(ang)
    cos2 = jnp.repeat(cos, 2, axis=-1)
    sin2 = jnp.stack([-sin, sin], axis=-1).reshape(n_tokens, HEAD_DIM)
    return cos2, sin2


def kernel(x, c, ctx, c_ctx, w_ada, b_ada, w_in, q_gain, k_gain, sgu_ln_g, sgu_ln_b, w_spatial,
           b_spatial, w_br_attn, w_br_sgu, w_br_fourier, w_out, ln1_g, ln1_b, w_up, w_down,
           ln2_g, ln2_b):
    batch, seq, d = x.shape
    n_ctx = ctx.shape[1]
    depth = w_in.shape[0]
    assert batch == 1 and d == D_MODEL and depth == DEPTH
    xl = x[0]
    xc = ctx[0]

    rope = _rope_tables(seq)
    tabs_x = _dft_tables(seq)
    tabs_c = _dft_tables(n_ctx)

    c2 = jnp.stack([c[0], c_ctx], axis=1)
    mod = _ada(c2, w_ada, b_ada)

    tm_x = _row_tile(seq, 1024)
    tm_c = _row_tile(n_ctx, 256)
    tm_merge_x = _row_tile(seq, 512)
    tq_x = _row_tile(seq, 256)
    tq_c = _row_tile(n_ctx, 256)

    row = lambda v: v.reshape(1, -1)
    kcol = CB_KV * COL_BLOCK // HEAD_DIM
    vcol = kcol + N_KV_HEADS

    for l in range(depth):
        last = l == depth - 1
        mx = [mod[l, 0:1, i * d:(i + 1) * d] for i in range(6)]
        mc = [mod[l, 1:2, i * d:(i + 1) * d] for i in range(6)]
        w_in_l = w_in[l].astype(BF16)
        qg, kg = row(q_gain[l]), row(k_gain[l])
        sg, sb = row(sgu_ln_g[l]), row(sgu_ln_b[l])
        wsp2 = w_spatial[l].astype(BF16).reshape(SGU_GROUPS // 2, 2 * SGU_CHUNK, SGU_CHUNK)
        bsp = jnp.repeat(b_spatial[l].T, SGU_WIDTH // SGU_GROUPS, axis=1)
        mixer_w = (w_br_attn[l].astype(BF16), w_br_sgu[l].astype(BF16), w_br_fourier[l].astype(BF16),
                   w_out[l].astype(BF16), wsp2, bsp)
        w_up_l, w_down_l = w_up[l].astype(BF16), w_down[l].astype(BF16)
        l1g, l1b, l2g, l2b = row(ln1_g[l]), row(ln1_b[l]), row(ln2_g[l]), row(ln2_b[l])

        if last:
            zc = _in_proj(xc, mc[0], mc[1], w_in_l, qg, kg, sg, sb, None, cb0=CB_KV, ncb=1, tm=tm_c)
            ctx_kv = (zc, 0, N_KV_HEADS)
        else:
            zc = _in_proj(xc, mc[0], mc[1], w_in_l, qg, kg, sg, sb, None, cb0=0, ncb=N_COL_BLOCKS, tm=tm_c)
            ctx_kv = (zc, kcol, vcol)
            ac = _attention(zc, ctx_kv, None, tq=tq_c)
            fc = _fourier(zc, tabs_c)
            xc1 = _merge(ac, zc, fc, xc, *mixer_w, mc[2], l1g, l1b, tm=tm_c)
            xc = _ffn(xc1, mc[3], mc[4], mc[5], w_up_l, w_down_l, l2g, l2b, tm=tm_c)

        zx = _in_proj(xl, mx[0], mx[1], w_in_l, qg, kg, sg, sb, rope, cb0=0, ncb=N_COL_BLOCKS, tm=tm_x)
        ax = _attention(zx, ctx_kv, (zx, kcol, vcol), tq=tq_x)
        fx = _fourier(zx, tabs_x)
        xl1 = _merge(ax, zx, fx, xl, *mixer_w, mx[2], l1g, l1b, tm=tm_merge_x)
        xl = _ffn(xl1, mx[3], mx[4], mx[5], w_up_l, w_down_l, l2g, l2b, tm=tm_merge_x)

    return xl[None]
```

```python
import functools
import math

import jax
import jax.numpy as jnp
import numpy as np
from jax import lax
from jax.experimental import pallas as pl
from jax.experimental.pallas import tpu as pltpu

F32 = jnp.float32
BF16 = jnp.bfloat16

D_MODEL = 1024
DEPTH = 2
GRID_W = 64
N_Q_HEADS = 8
N_KV_HEADS = 2
HEAD_DIM = 128
Q_PER_KV = N_Q_HEADS // N_KV_HEADS
ATTN_WIDTH = N_Q_HEADS * HEAD_DIM
KV_WIDTH = N_KV_HEADS * HEAD_DIM
ROPE_AXIS_DIM = HEAD_DIM // 2
ROPE_THETA = 10000.0
SGU_GROUPS = 8
SGU_WIDTH = 512
SGU_CHUNK = 128
FOURIER_GROUPS = 4
FOURIER_WIDTH = 512
FOURIER_GROUP_CH = FOURIER_WIDTH // FOURIER_GROUPS
N_BRANCHES = 3
FFN_HIDDEN = -(-8 * D_MODEL // (3 * 256)) * 256
IN_WIDTH = ATTN_WIDTH + 2 * KV_WIDTH + 2 * SGU_WIDTH + FOURIER_WIDTH + N_BRANCHES * D_MODEL
DEEPNORM_ALPHA = (2 * DEPTH) ** 0.25
LN_EPS = 1e-6
RMS_EPS = 1e-6
ATTN_SCALE = HEAD_DIM ** -0.5
Q_PRESCALE = ATTN_SCALE * math.log2(math.e)

COL_BLOCK = 512
N_COL_BLOCKS = IN_WIDTH // COL_BLOCK
CB_Q0 = 0
CB_KV = 2
CB_SGU_U = 3
CB_SGU_V = 4
CB_FOURIER = 5
CB_GATES = 6

LANES = 128
VMEM_LIMIT_BYTES = 52 * 1024 * 1024

ADA_TN = 1024
ATTN_BK = 512
ATTN_ROW_CHUNK = 256
ATTN_UNROLL = 4
FFN_CHUNKS = 2


def _cparams(sem):
    return pltpu.CompilerParams(dimension_semantics=sem, vmem_limit_bytes=VMEM_LIMIT_BYTES)


def _row_tile(t, want):
    tm = min(t, want)
    assert t % tm == 0
    return tm


def _sigmoid(x):
    return 1.0 / (1.0 + jnp.exp(-x))


def _gelu_tanh(x):
    c = math.sqrt(2.0 / math.pi)
    return 0.5 * x * (1.0 + jnp.tanh(c * (x + 0.044715 * (x * x * x))))


def _ln(x):
    mu = jnp.mean(x, axis=-1, keepdims=True)
    xc = x - mu
    var = jnp.mean(xc * xc, axis=-1, keepdims=True)
    return xc * lax.rsqrt(var + LN_EPS)


def _ada_kernel(c_ref, w_ref, b_ref, o_ref):
    c = c_ref[...]
    s = c * _sigmoid(c)
    w = w_ref[0]
    b = b_ref[0]
    r0 = jnp.sum(w * s[:, 0:1], axis=0, keepdims=True) + b
    r1 = jnp.sum(w * s[:, 1:2], axis=0, keepdims=True) + b
    o_ref[0] = jnp.concatenate([r0, r1], axis=0)


def _ada(c2, w_ada, b_ada):
    depth, d, n = w_ada.shape
    tn = ADA_TN
    return pl.pallas_call(
        _ada_kernel,
        grid=(depth, n // tn),
        in_specs=[
            pl.BlockSpec((d, 2), lambda l, j: (0, 0)),
            pl.BlockSpec((1, d, tn), lambda l, j: (l, 0, j)),
            pl.BlockSpec((1, 1, tn), lambda l, j: (l, 0, j)),
        ],
        out_specs=pl.BlockSpec((1, 2, tn), lambda l, j: (l, 0, j)),
        out_shape=jax.ShapeDtypeStruct((depth, 2, n), F32),
        compiler_params=_cparams(("arbitrary", "arbitrary")),
        name="ada",
    )(c2, w_ada, b_ada.reshape(depth, 1, n))


def _rms_rope(zh, gain, cos, sin_signed, use_rope):
    y = zh * lax.rsqrt(jnp.mean(zh * zh, axis=-1, keepdims=True) + RMS_EPS) * gain
    if not use_rope:
        return y
    lane = lax.broadcasted_iota(jnp.int32, y.shape, 1)
    nxt = pltpu.roll(y, HEAD_DIM - 1, axis=1)
    prv = pltpu.roll(y, 1, axis=1)
    partner = jnp.where((lane & 1) == 0, nxt, prv)
    return y * cos + partner * sin_signed


def _in_kernel(x_ref, sh_ref, sc_ref, w_ref, qg_ref, kg_ref, lg_ref, lb_ref, cos_ref, sin_ref,
               o_ref, *, cb0, ncb, use_rope):
    h = (_ln(x_ref[...]) * (1.0 + sc_ref[...]) + sh_ref[...]).astype(BF16)
    cos = cos_ref[...] if use_rope else None
    sin = sin_ref[...] if use_rope else None

    def qk_heads(z, base, count, gain):
        for hh in range(count):
            sl = slice(hh * HEAD_DIM, (hh + 1) * HEAD_DIM)
            o_ref[:, base + hh * HEAD_DIM:base + (hh + 1) * HEAD_DIM] = _rms_rope(
                z[:, sl], gain, cos, sin, use_rope).astype(BF16)

    for jj in range(ncb):
        cb = cb0 + jj
        base = jj * COL_BLOCK
        cols = slice(base, base + COL_BLOCK)
        z = jnp.dot(h, w_ref[:, cols], preferred_element_type=F32)
        if cb < CB_KV:
            qk_heads(z, base, COL_BLOCK // HEAD_DIM, qg_ref[...] * Q_PRESCALE)
        elif cb == CB_KV:
            qk_heads(z, base, N_KV_HEADS, kg_ref[...])
            o_ref[:, base + KV_WIDTH:base + COL_BLOCK] = z[:, KV_WIDTH:].astype(BF16)
        elif cb == CB_SGU_U:
            o_ref[:, cols] = _gelu_tanh(z).astype(BF16)
        elif cb == CB_SGU_V:
            o_ref[:, cols] = (_ln(_gelu_tanh(z)) * lg_ref[...] + lb_ref[...]).astype(BF16)
        elif cb == CB_FOURIER:
            o_ref[:, cols] = z.astype(BF16)
        else:
            o_ref[:, cols] = _sigmoid(z).astype(BF16)


def _in_proj(x, shift, scale, w_in, q_gain, k_gain, sgu_g, sgu_b, rope, *, cb0, ncb, tm):
    t, d = x.shape
    assert cb0 % ncb == 0
    width = ncb * COL_BLOCK
    use_rope = rope is not None
    if use_rope:
        cos, sin = rope
    else:
        cos = sin = jnp.zeros((8, HEAD_DIM), F32)
    rope_spec = (pl.BlockSpec((tm, HEAD_DIM), lambda i: (i, 0)) if use_rope
                 else pl.BlockSpec((8, HEAD_DIM), lambda i: (0, 0)))
    vec = lambda n: pl.BlockSpec((1, n), lambda i: (0, 0))
    return pl.pallas_call(
        functools.partial(_in_kernel, cb0=cb0, ncb=ncb, use_rope=use_rope),
        grid=(t // tm,),
        in_specs=[
            pl.BlockSpec((tm, d), lambda i: (i, 0)),
            vec(d), vec(d),
            pl.BlockSpec((d, width), lambda i: (0, cb0 // ncb), pipeline_mode=pl.Buffered(1)),
            vec(HEAD_DIM), vec(HEAD_DIM), vec(SGU_WIDTH), vec(SGU_WIDTH),
            rope_spec, rope_spec,
        ],
        out_specs=pl.BlockSpec((tm, width), lambda i: (i, 0)),
        out_shape=jax.ShapeDtypeStruct((t, width), BF16),
        compiler_params=_cparams(("arbitrary",)),
        name="in_proj",
    )(x, shift, scale, w_in, q_gain, k_gain, sgu_g, sgu_b, cos, sin)


def _attn_kernel(*refs, tq, bk, n_main, bkc):
    if n_main:
        q_ref, kc_ref, vc_ref, k_ref, v_ref, o_ref = refs[:6]
        qs_ref, s0_ref, s1_ref, p0_ref, p1_ref, m_ref, l_ref, acc_ref = refs[6:]
    else:
        q_ref, kc_ref, vc_ref, o_ref = refs[:4]
        qs_ref, s1_ref, p1_ref, m_ref, l_ref, acc_ref = refs[4:]
    m_rows = Q_PER_KV * tq
    rc = min(ATTN_ROW_CHUNK, m_rows)

    for g in range(Q_PER_KV):
        qs_ref[g * tq:(g + 1) * tq, :] = q_ref[:, g * HEAD_DIM:(g + 1) * HEAD_DIM]

    def scores(kblk):
        return lax.dot_general(qs_ref[...], kblk, (((1,), (1,)), ((), ())),
                               preferred_element_type=F32)

    def softmax_pv(s_ref, p_ref, vblk, ncols, first):
        for r0 in range(0, m_rows, rc):
            rows = slice(r0, r0 + rc)
            cols = [s_ref[rows, c * LANES:(c + 1) * LANES] for c in range(ncols)]
            mx = cols[0]
            for c in cols[1:]:
                mx = jnp.maximum(mx, c)
            m_cur = jnp.max(mx, axis=-1, keepdims=True)
            if first:
                m_new = jnp.broadcast_to(m_cur, (rc, LANES))
            else:
                m_prev = m_ref[rows, :]
                m_new = jnp.maximum(m_prev, m_cur)
            ps = [jnp.exp2(c - m_new) for c in cols]
            psum = ps[0]
            for p in ps[1:]:
                psum = psum + p
            for c in range(ncols):
                p_ref[rows, c * LANES:(c + 1) * LANES] = ps[c].astype(BF16)
            if first:
                l_ref[rows, :] = psum
            else:
                alpha = jnp.exp2(m_prev - m_new)
                l_ref[rows, :] = alpha * l_ref[rows, :] + psum
                acc_ref[rows, :] = alpha * acc_ref[rows, :]
            m_ref[rows, :] = m_new
        pv = jnp.dot(p_ref[:, :ncols * LANES], vblk, preferred_element_type=F32)
        if first:
            acc_ref[...] = pv
        else:
            acc_ref[...] += pv

    def kblock(ref, i):
        return ref[pl.ds(pl.multiple_of(i * bk, bk), bk), :]

    if n_main:
        s0_ref[...] = scores(k_ref[0:bk, :])
    s1_ref[:, :bkc] = scores(kc_ref[...])
    softmax_pv(s1_ref, p1_ref, vc_ref[...], bkc // LANES, True)

    if n_main:
        ncols = bk // LANES
        s_bufs, p_bufs = (s0_ref, s1_ref), (p0_ref, p1_ref)
        unroll = ATTN_UNROLL
        assert unroll % 2 == 0 and n_main % unroll == 0

        def group(j, produce_last):
            for u in range(unroll):
                b = unroll * j + u
                if u + 1 < unroll or produce_last:
                    s_bufs[(u + 1) % 2][...] = scores(kblock(k_ref, b + 1))
                softmax_pv(s_bufs[u % 2], p_bufs[u % 2], kblock(v_ref, b), ncols, False)

        def body(j, carry):
            group(j, True)
            return carry

        lax.fori_loop(0, n_main // unroll - 1, body, 0)
        group(n_main // unroll - 1, False)

    l_row = jnp.sum(l_ref[...], axis=-1, keepdims=True)
    o = acc_ref[...] / l_row
    for g in range(Q_PER_KV):
        o_ref[:, g * HEAD_DIM:(g + 1) * HEAD_DIM] = o[g * tq:(g + 1) * tq, :].astype(BF16)


def _attention(zq, ctx_kv, main_kv, *, tq):
    t = zq.shape[0]
    m_rows = Q_PER_KV * tq
    kv_spec = lambda length, col: pl.BlockSpec((length, HEAD_DIM), lambda h, i: (0, col + h))
    zc, kcol_c, vcol_c = ctx_kv
    bkc = zc.shape[0]
    assert bkc % LANES == 0
    args = [zq, zc, zc]
    in_specs = [pl.BlockSpec((tq, Q_PER_KV * HEAD_DIM), lambda h, i: (i, h)),
                kv_spec(bkc, kcol_c), kv_spec(bkc, vcol_c)]
    scratch = [pltpu.VMEM((m_rows, HEAD_DIM), BF16)]
    if main_kv is not None:
        zm, kcol_m, vcol_m = main_kv
        bk = ATTN_BK
        n_main = zm.shape[0] // bk
        assert n_main * bk == zm.shape[0] and bkc <= bk
        args += [zm, zm]
        in_specs += [kv_spec(zm.shape[0], kcol_m), kv_spec(zm.shape[0], vcol_m)]
        scratch += [pltpu.VMEM((m_rows, bk), F32), pltpu.VMEM((m_rows, bk), F32),
                    pltpu.VMEM((m_rows, bk), BF16), pltpu.VMEM((m_rows, bk), BF16)]
    else:
        bk, n_main = bkc, 0
        scratch += [pltpu.VMEM((m_rows, bkc), F32), pltpu.VMEM((m_rows, bkc), BF16)]
    scratch += [pltpu.VMEM((m_rows, LANES), F32), pltpu.VMEM((m_rows, LANES), F32),
                pltpu.VMEM((m_rows, HEAD_DIM), F32)]
    return pl.pallas_call(
        functools.partial(_attn_kernel, tq=tq, bk=bk, n_main=n_main, bkc=bkc),
        grid=(N_KV_HEADS, t // tq),
        in_specs=in_specs,
        out_specs=pl.BlockSpec((tq, Q_PER_KV * HEAD_DIM), lambda h, i: (i, h)),
        out_shape=jax.ShapeDtypeStruct((t, ATTN_WIDTH), BF16),
        scratch_shapes=scratch,
        compiler_params=_cparams(("arbitrary", "arbitrary")),
        name="attn",
    )(*args)


def _dft_tables(t):
    n2 = min(t, 128) if t > 256 else t
    n1 = t // n2
    assert n1 * n2 == t

    def cs(n):
        idx = np.arange(n)
        ang = 2.0 * np.pi * ((idx[:, None] * idx[None, :]) % n) / n
        return np.cos(ang), np.sin(ang)

    tabs = {"n1": n1, "n2": n2}
    c2, s2 = cs(n2)
    if n1 > 1:
        c1, s1 = cs(n1)
        tabs["m1"] = jnp.asarray(np.concatenate([c1, -s1], axis=0), F32)
        ang = 2.0 * np.pi * (np.arange(n1)[:, None] * np.arange(n2)[None, :]) / t
        tabs["twr"] = jnp.asarray(np.repeat(np.cos(ang), LANES, axis=1), F32)
        tabs["twi"] = jnp.asarray(np.repeat(-np.sin(ang), LANES, axis=1), F32)
        m2 = np.block([[c2, s2], [-s2, c2]])
    else:
        m2 = np.concatenate([c2, -s2], axis=0)
    tabs["m2"] = jnp.asarray(m2, F32)
    cc, sc = cs(FOURIER_GROUP_CH)
    tabs["mc"] = jnp.asarray(np.concatenate([cc, sc], axis=0), F32)
    tabs["norm"] = 1.0 / math.sqrt(t * FOURIER_GROUP_CH)
    return tabs


def _dft_a_kernel(f_ref, m1_ref, twr_ref, twi_ref, xr_ref, xi_ref, *, n1, nt2):
    w = FOURIER_WIDTH
    for j in range(nt2):
        x1 = jnp.dot(m1_ref[...], f_ref[:, j * w:(j + 1) * w], preferred_element_type=F32)
        wr = twr_ref[:, j * LANES:(j + 1) * LANES]
        wi = twi_ref[:, j * LANES:(j + 1) * LANES]
        for g in range(w // LANES):
            a = x1[:n1, g * LANES:(g + 1) * LANES]
            b = x1[n1:, g * LANES:(g + 1) * LANES]
            sl = slice(j * w + g * LANES, j * w + (g + 1) * LANES)
            xr_ref[:, sl] = (a * wr - b * wi).astype(BF16)
            xi_ref[:, sl] = (a * wi + b * wr).astype(BF16)


def _dft_b_kernel(*refs, n2, kb, complex_in, norm):
    if complex_in:
        xr_ref, xi_ref, m2_ref, mc_ref, o_ref = refs
    else:
        xr_ref, m2_ref, mc_ref, o_ref = refs
    c = FOURIER_GROUP_CH
    for j in range(kb):
        if complex_in:
            rhs = jnp.concatenate([xr_ref[j], xi_ref[j]], axis=0)
        else:
            rhs = xr_ref[j]
        p = jnp.dot(m2_ref[...], rhs, preferred_element_type=F32)
        for g in range(FOURIER_GROUPS):
            pg = jnp.concatenate([p[:n2, g * c:(g + 1) * c], p[n2:, g * c:(g + 1) * c]], axis=1)
            y = jnp.dot(pg.astype(BF16), mc_ref[...], preferred_element_type=F32) * norm
            o_ref[:, j, g * c:(g + 1) * c] = y.astype(BF16)


def _fourier(z, tabs):
    t = z.shape[0]
    n1, n2, w = tabs["n1"], tabs["n2"], FOURIER_WIDTH
    if n1 > 1:
        f = z[:, CB_FOURIER * COL_BLOCK:(CB_FOURIER + 1) * COL_BLOCK].reshape(n1, n2 * w)
        nt2 = 8
        assert n2 % nt2 == 0
        xr, xi = pl.pallas_call(
            functools.partial(_dft_a_kernel, n1=n1, nt2=nt2),
            grid=(n2 // nt2,),
            in_specs=[
                pl.BlockSpec((n1, nt2 * w), lambda i: (0, i)),
                pl.BlockSpec((2 * n1, n1), lambda i: (0, 0)),
                pl.BlockSpec((n1, nt2 * LANES), lambda i: (0, i)),
                pl.BlockSpec((n1, nt2 * LANES), lambda i: (0, i)),
            ],
            out_specs=[pl.BlockSpec((n1, nt2 * w), lambda i: (0, i))] * 2,
            out_shape=[jax.ShapeDtypeStruct((n1, n2 * w), BF16)] * 2,
            compiler_params=_cparams(("arbitrary",)),
            name="dft_a",
        )(f, tabs["m1"].astype(BF16), tabs["twr"], tabs["twi"])
        kb = 8
        assert n1 % kb == 0
        xs = [xr.reshape(n1, n2, w), xi.reshape(n1, n2, w)]
        x_specs = [pl.BlockSpec((kb, n2, w), lambda i: (i, 0, 0))] * 2
        complex_in = True
    else:
        kb = 1
        xs = [z.reshape(1, t, z.shape[1])]
        x_specs = [pl.BlockSpec((1, n2, w), lambda i: (0, 0, CB_FOURIER))]
        complex_in = False
    m2, mc = tabs["m2"].astype(BF16), tabs["mc"].astype(BF16)
    out = pl.pallas_call(
        functools.partial(_dft_b_kernel, n2=n2, kb=kb, complex_in=complex_in, norm=tabs["norm"]),
        grid=(n1 // kb,),
        in_specs=x_specs + [
            pl.BlockSpec(m2.shape, lambda i: (0, 0)),
            pl.BlockSpec(mc.shape, lambda i: (0, 0)),
        ],
        out_specs=pl.BlockSpec((n2, kb, w), lambda i: (0, i, 0)),
        out_shape=jax.ShapeDtypeStruct((n2, n1, w), BF16),
        compiler_params=_cparams(("arbitrary",)),
        name="dft_b",
    )(*xs, m2, mc)
    return out.reshape(t, w)


def _merge_kernel(a_ref, u_ref, vn_ref, fo_ref, g_ref, x_ref, wa_ref, ws_ref, wf_ref, wo_ref,
                  wsp_ref, bsp_ref, gate_ref, lg_ref, lb_ref, o_ref, gout_ref, *, tm):
    group_ch = SGU_WIDTH // SGU_GROUPS
    lane = lax.broadcasted_iota(jnp.int32, (SGU_CHUNK, LANES), 1)
    for ci in range(tm // SGU_CHUNK):
        rows = slice(ci * SGU_CHUNK, (ci + 1) * SGU_CHUNK)
        for lb in range(SGU_WIDTH // LANES):
            cols = slice(lb * LANES, (lb + 1) * LANES)
            y = jnp.dot(wsp_ref[lb], vn_ref[rows, cols], preferred_element_type=F32)
            mixed = jnp.where(lane < group_ch, y[:SGU_CHUNK], y[SGU_CHUNK:]) + bsp_ref[:, cols]
            gout_ref[rows, cols] = (u_ref[rows, cols].astype(F32) * mixed).astype(BF16)

    d = D_MODEL
    merged = g_ref[:, 0:d].astype(F32) * jnp.dot(a_ref[...], wa_ref[...], preferred_element_type=F32)
    merged += g_ref[:, d:2 * d].astype(F32) * jnp.dot(gout_ref[...], ws_ref[...],
                                                      preferred_element_type=F32)
    merged += g_ref[:, 2 * d:3 * d].astype(F32) * jnp.dot(fo_ref[...], wf_ref[...],
                                                          preferred_element_type=F32)
    mix = jnp.dot(merged.astype(BF16), wo_ref[...], preferred_element_type=F32)
    y = DEEPNORM_ALPHA * x_ref[...] + gate_ref[...] * mix
    o_ref[...] = _ln(y) * lg_ref[...] + lb_ref[...]


def _merge(a, z, f_out, x, w_br_attn, w_br_sgu, w_br_fourier, w_out, wsp2, bsp, gate1, ln_g, ln_b, *, tm):
    t, d = x.shape
    full = lambda arr: pl.BlockSpec(arr.shape, lambda i: (0,) * arr.ndim)
    vec = pl.BlockSpec((1, d), lambda i: (0, 0))
    return pl.pallas_call(
        functools.partial(_merge_kernel, tm=tm),
        grid=(t // tm,),
        in_specs=[
            pl.BlockSpec((tm, ATTN_WIDTH), lambda i: (i, 0)),
            pl.BlockSpec((tm, COL_BLOCK), lambda i: (i, CB_SGU_U)),
            pl.BlockSpec((tm, COL_BLOCK), lambda i: (i, CB_SGU_V)),
            pl.BlockSpec((tm, FOURIER_WIDTH), lambda i: (i, 0)),
            pl.BlockSpec((tm, N_BRANCHES * d), lambda i: (i, CB_GATES * COL_BLOCK // (N_BRANCHES * d))),
            pl.BlockSpec((tm, d), lambda i: (i, 0)),
            full(w_br_attn), full(w_br_sgu), full(w_br_fourier), full(w_out),
            full(wsp2), full(bsp), vec, vec, vec,
        ],
        out_specs=pl.BlockSpec((tm, d), lambda i: (i, 0)),
        out_shape=jax.ShapeDtypeStruct((t, d), F32),
        scratch_shapes=[pltpu.VMEM((tm, SGU_WIDTH), BF16)],
        compiler_params=_cparams(("arbitrary",)),
        name="merge",
    )(a, z, z, f_out, z, x, w_br_attn, w_br_sgu, w_br_fourier, w_out, wsp2, bsp, gate1, ln_g, ln_b)


def _ffn_kernel(x_ref, sh_ref, sc_ref, gate_ref, wg_ref, wu_ref, wd_ref, lg_ref, lb_ref,
                o_ref, h_ref, acc_ref):
    c = pl.program_id(1)

    @pl.when(c == 0)
    def _():
        h = _ln(x_ref[...]) * (1.0 + sc_ref[...]) + sh_ref[...]
        h_ref[...] = h.astype(BF16)
        acc_ref[...] = jnp.zeros(acc_ref.shape, F32)

    h = h_ref[...]
    g = jnp.dot(h, wg_ref[...], preferred_element_type=F32)
    u = jnp.dot(h, wu_ref[...], preferred_element_type=F32)
    act = (g * _sigmoid(g) * u).astype(BF16)
    acc_ref[...] += jnp.dot(act, wd_ref[...], preferred_element_type=F32)

    @pl.when(c == pl.num_programs(1) - 1)
    def _():
        y = DEEPNORM_ALPHA * x_ref[...] + gate_ref[...] * acc_ref[...]
        o_ref[...] = _ln(y) * lg_ref[...] + lb_ref[...]


def _ffn(x, shift, scale, gate, w_up, w_down, ln_g, ln_b, *, tm):
    t, d = x.shape
    hidden = w_down.shape[0]
    th = hidden // FFN_CHUNKS
    assert th * FFN_CHUNKS == hidden and th % LANES == 0
    vec = pl.BlockSpec((1, d), lambda i, c: (0, 0))
    return pl.pallas_call(
        _ffn_kernel,
        grid=(t // tm, FFN_CHUNKS),
        in_specs=[
            pl.BlockSpec((tm, d), lambda i, c: (i, 0)),
            vec, vec, vec,
            pl.BlockSpec((d, th), lambda i, c: (0, c)),
            pl.BlockSpec((d, th), lambda i, c: (0, c + FFN_CHUNKS)),
            pl.BlockSpec((th, d), lambda i, c: (c, 0)),
            vec, vec,
        ],
        out_specs=pl.BlockSpec((tm, d), lambda i, c: (i, 0)),
        out_shape=jax.ShapeDtypeStruct((t, d), F32),
        scratch_shapes=[pltpu.VMEM((tm, d), BF16), pltpu.VMEM((tm, d), F32)],
        compiler_params=_cparams(("arbitrary", "arbitrary")),
        name="ffn",
    )(x, shift, scale, gate, w_up, w_up, w_down, ln_g, ln_b)


def _rope_tables(n_tokens):
    rows = n_tokens // GRID_W
    pos_r = jnp.repeat(jnp.arange(rows, dtype=F32), GRID_W)
    pos_c = jnp.tile(jnp.arange(GRID_W, dtype=F32), rows)
    inv = ROPE_THETA ** (-jnp.arange(0, ROPE_AXIS_DIM, 2, dtype=F32) / ROPE_AXIS_DIM)
    ang = jnp.concatenate([pos_r[:, None] * inv, pos_c[:, None] * inv], axis=-1)
    cos, sin = jnp.cos(ang), jnp.sin(ang)
    cos2 = jnp.repeat(cos, 2, axis=-1)
    sin2 = jnp.stack([-sin, sin], axis=-1).reshape(n_tokens, HEAD_DIM)
    return cos2, sin2


def kernel(x, c, ctx, c_ctx, w_ada, b_ada, w_in, q_gain, k_gain, sgu_ln_g, sgu_ln_b, w_spatial,
           b_spatial, w_br_attn, w_br_sgu, w_br_fourier, w_out, ln1_g, ln1_b, w_up, w_down,
           ln2_g, ln2_b):
    batch, seq, d = x.shape
    n_ctx = ctx.shape[1]
    depth = w_in.shape[0]
    assert batch == 1 and d == D_MODEL and depth == DEPTH
    xl = x[0]
    xc = ctx[0]

    rope = _rope_tables(seq)
    tabs_x = _dft_tables(seq)
    tabs_c = _dft_tables(n_ctx)

    c2 = jnp.stack([c[0], c_ctx], axis=1)
    mod = _ada(c2, w_ada, b_ada)

    tm_x = _row_tile(seq, 512)
    tm_c = _row_tile(n_ctx, 256)
    tm_merge_x = _row_tile(seq, 512)
    tq_x = _row_tile(seq, 256)
    tq_c = _row_tile(n_ctx, 256)

    row = lambda v: v.reshape(1, -1)
    kcol = CB_KV * COL_BLOCK // HEAD_DIM
    vcol = kcol + N_KV_HEADS

    for l in range(depth):
        last = l == depth - 1
        mx = [mod[l, 0:1, i * d:(i + 1) * d] for i in range(6)]
        mc = [mod[l, 1:2, i * d:(i + 1) * d] for i in range(6)]
        w_in_l = w_in[l].astype(BF16)
        qg, kg = row(q_gain[l]), row(k_gain[l])
        sg, sb = row(sgu_ln_g[l]), row(sgu_ln_b[l])
        wsp2 = w_spatial[l].astype(BF16).reshape(SGU_GROUPS // 2, 2 * SGU_CHUNK, SGU_CHUNK)
        bsp = jnp.repeat(b_spatial[l].T, SGU_WIDTH // SGU_GROUPS, axis=1)
        mixer_w = (w_br_attn[l].astype(BF16), w_br_sgu[l].astype(BF16), w_br_fourier[l].astype(BF16),
                   w_out[l].astype(BF16), wsp2, bsp)
        w_up_l, w_down_l = w_up[l].astype(BF16), w_down[l].astype(BF16)
        l1g, l1b, l2g, l2b = row(ln1_g[l]), row(ln1_b[l]), row(ln2_g[l]), row(ln2_b[l])

        if last:
            zc = _in_proj(xc, mc[0], mc[1], w_in_l, qg, kg, sg, sb, None, cb0=CB_KV, ncb=1, tm=tm_c)
            ctx_kv = (zc, 0, N_KV_HEADS)
        else:
            zc = _in_proj(xc, mc[0], mc[1], w_in_l, qg, kg, sg, sb, None, cb0=0, ncb=N_COL_BLOCKS, tm=tm_c)
            ctx_kv = (zc, kcol, vcol)
            ac = _attention(zc, ctx_kv, None, tq=tq_c)
            fc = _fourier(zc, tabs_c)
            xc1 = _merge(ac, zc, fc, xc, *mixer_w, mc[2], l1g, l1b, tm=tm_c)
            xc = _ffn(xc1, mc[3], mc[4], mc[5], w_up_l, w_down_l, l2g, l2b, tm=tm_c)

        zx = _in_proj(xl, mx[0], mx[1], w_in_l, qg, kg, sg, sb, rope, cb0=0, ncb=N_COL_BLOCKS, tm=tm_x)
        ax = _attention(zx, ctx_kv, (zx, kcol, vcol), tq=tq_x)
        fx = _fourier(zx, tabs_x)
        xl1 = _merge(ax, zx, fx, xl, *mixer_w, mx[2], l1g, l1b, tm=tm_merge_x)
        xl = _ffn(xl1, mx[3], mx[4], mx[5], w_up_l, w_down_l, l2g, l2b, tm=tm_merge_x)

    return xl[None]
```

```python
import functools
import math

import jax
import jax.numpy as jnp
import numpy as np
from jax import lax
from jax.experimental import pallas as pl
from jax.experimental.pallas import tpu as pltpu

F32 = jnp.float32
BF16 = jnp.bfloat16

D_MODEL = 1024
DEPTH = 2
GRID_W = 64
N_Q_HEADS = 8
N_KV_HEADS = 2
HEAD_DIM = 128
Q_PER_KV = N_Q_HEADS // N_KV_HEADS
ATTN_WIDTH = N_Q_HEADS * HEAD_DIM
KV_WIDTH = N_KV_HEADS * HEAD_DIM
ROPE_AXIS_DIM = HEAD_DIM // 2
ROPE_THETA = 10000.0
SGU_GROUPS = 8
SGU_WIDTH = 512
SGU_CHUNK = 128
FOURIER_GROUPS = 4
FOURIER_WIDTH = 512
FOURIER_GROUP_CH = FOURIER_WIDTH // FOURIER_GROUPS
N_BRANCHES = 3
FFN_HIDDEN = -(-8 * D_MODEL // (3 * 256)) * 256
IN_WIDTH = ATTN_WIDTH + 2 * KV_WIDTH + 2 * SGU_WIDTH + FOURIER_WIDTH + N_BRANCHES * D_MODEL
DEEPNORM_ALPHA = (2 * DEPTH) ** 0.25
LN_EPS = 1e-6
RMS_EPS = 1e-6
ATTN_SCALE = HEAD_DIM ** -0.5
Q_PRESCALE = ATTN_SCALE * math.log2(math.e)

COL_BLOCK = 512
N_COL_BLOCKS = IN_WIDTH // COL_BLOCK
CB_Q0 = 0
CB_KV = 2
CB_SGU_U = 3
CB_SGU_V = 4
CB_FOURIER = 5
CB_GATES = 6

LANES = 128
VMEM_LIMIT_BYTES = 52 * 1024 * 1024

ADA_TN = 1024
ATTN_BK = 512
ATTN_ROW_CHUNK = 256
ATTN_UNROLL = 8
FFN_CHUNKS = 2


def _cparams(sem):
    return pltpu.CompilerParams(dimension_semantics=sem, vmem_limit_bytes=VMEM_LIMIT_BYTES)


def _row_tile(t, want):
    tm = min(t, want)
    assert t % tm == 0
    return tm


def _sigmoid(x):
    return 1.0 / (1.0 + jnp.exp(-x))


def _gelu_tanh(x):
    c = math.sqrt(2.0 / math.pi)
    return 0.5 * x * (1.0 + jnp.tanh(c * (x + 0.044715 * (x * x * x))))


def _ln(x):
    mu = jnp.mean(x, axis=-1, keepdims=True)
    xc = x - mu
    var = jnp.mean(xc * xc, axis=-1, keepdims=True)
    return xc * lax.rsqrt(var + LN_EPS)


def _ada_kernel(c_ref, w_ref, b_ref, o_ref):
    c = c_ref[...]
    s = c * _sigmoid(c)
    w = w_ref[0]
    b = b_ref[0]
    r0 = jnp.sum(w * s[:, 0:1], axis=0, keepdims=True) + b
    r1 = jnp.sum(w * s[:, 1:2], axis=0, keepdims=True) + b
    o_ref[0] = jnp.concatenate([r0, r1], axis=0)


def _ada(c2, w_ada, b_ada):
    depth, d, n = w_ada.shape
    tn = ADA_TN
    return pl.pallas_call(
        _ada_kernel,
        grid=(depth, n // tn),
        in_specs=[
            pl.BlockSpec((d, 2), lambda l, j: (0, 0)),
            pl.BlockSpec((1, d, tn), lambda l, j: (l, 0, j)),
            pl.BlockSpec((1, 1, tn), lambda l, j: (l, 0, j)),
        ],
        out_specs=pl.BlockSpec((1, 2, tn), lambda l, j: (l, 0, j)),
        out_shape=jax.ShapeDtypeStruct((depth, 2, n), F32),
        compiler_params=_cparams(("arbitrary", "arbitrary")),
        name="ada",
    )(c2, w_ada, b_ada.reshape(depth, 1, n))


def _rms_rope(zh, gain, cos, sin_signed, use_rope):
    y = zh * lax.rsqrt(jnp.mean(zh * zh, axis=-1, keepdims=True) + RMS_EPS) * gain
    if not use_rope:
        return y
    lane = lax.broadcasted_iota(jnp.int32, y.shape, 1)
    nxt = pltpu.roll(y, HEAD_DIM - 1, axis=1)
    prv = pltpu.roll(y, 1, axis=1)
    partner = jnp.where((lane & 1) == 0, nxt, prv)
    return y * cos + partner * sin_signed


def _in_kernel(x_ref, sh_ref, sc_ref, w_ref, qg_ref, kg_ref, lg_ref, lb_ref, cos_ref, sin_ref,
               o_ref, *, cb0, ncb, use_rope):
    h = (_ln(x_ref[...]) * (1.0 + sc_ref[...]) + sh_ref[...]).astype(BF16)
    cos = cos_ref[...] if use_rope else None
    sin = sin_ref[...] if use_rope else None

    def qk_heads(z, base, count, gain):
        for hh in range(count):
            sl = slice(hh * HEAD_DIM, (hh + 1) * HEAD_DIM)
            o_ref[:, base + hh * HEAD_DIM:base + (hh + 1) * HEAD_DIM] = _rms_rope(
                z[:, sl], gain, cos, sin, use_rope).astype(BF16)

    for jj in range(ncb):
        cb = cb0 + jj
        base = jj * COL_BLOCK
        cols = slice(base, base + COL_BLOCK)
        z = jnp.dot(h, w_ref[:, cols], preferred_element_type=F32)
        if cb < CB_KV:
            qk_heads(z, base, COL_BLOCK // HEAD_DIM, qg_ref[...] * Q_PRESCALE)
        elif cb == CB_KV:
            qk_heads(z, base, N_KV_HEADS, kg_ref[...])
            o_ref[:, base + KV_WIDTH:base + COL_BLOCK] = z[:, KV_WIDTH:].astype(BF16)
        elif cb == CB_SGU_U:
            o_ref[:, cols] = _gelu_tanh(z).astype(BF16)
        elif cb == CB_SGU_V:
            o_ref[:, cols] = (_ln(_gelu_tanh(z)) * lg_ref[...] + lb_ref[...]).astype(BF16)
        elif cb == CB_FOURIER:
            o_ref[:, cols] = z.astype(BF16)
        else:
            o_ref[:, cols] = _sigmoid(z).astype(BF16)


def _in_proj(x, shift, scale, w_in, q_gain, k_gain, sgu_g, sgu_b, rope, *, cb0, ncb, tm):
    t, d = x.shape
    assert cb0 % ncb == 0
    width = ncb * COL_BLOCK
    use_rope = rope is not None
    if use_rope:
        cos, sin = rope
    else:
        cos = sin = jnp.zeros((8, HEAD_DIM), F32)
    rope_spec = (pl.BlockSpec((tm, HEAD_DIM), lambda i: (i, 0)) if use_rope
                 else pl.BlockSpec((8, HEAD_DIM), lambda i: (0, 0)))
    vec = lambda n: pl.BlockSpec((1, n), lambda i: (0, 0))
    return pl.pallas_call(
        functools.partial(_in_kernel, cb0=cb0, ncb=ncb, use_rope=use_rope),
        grid=(t // tm,),
        in_specs=[
            pl.BlockSpec((tm, d), lambda i: (i, 0)),
            vec(d), vec(d),
            pl.BlockSpec((d, width), lambda i: (0, cb0 // ncb), pipeline_mode=pl.Buffered(1)),
            vec(HEAD_DIM), vec(HEAD_DIM), vec(SGU_WIDTH), vec(SGU_WIDTH),
            rope_spec, rope_spec,
        ],
        out_specs=pl.BlockSpec((tm, width), lambda i: (i, 0)),
        out_shape=jax.ShapeDtypeStruct((t, width), BF16),
        compiler_params=_cparams(("arbitrary",)),
        name="in_proj",
    )(x, shift, scale, w_in, q_gain, k_gain, sgu_g, sgu_b, cos, sin)


def _attn_kernel(*refs, tq, bk, n_main, bkc):
    if n_main:
        q_ref, kc_ref, vc_ref, k_ref, v_ref, o_ref = refs[:6]
        qs_ref, s0_ref, s1_ref, p0_ref, p1_ref, m_ref, l_ref, acc_ref = refs[6:]
    else:
        q_ref, kc_ref, vc_ref, o_ref = refs[:4]
        qs_ref, s1_ref, p1_ref, m_ref, l_ref, acc_ref = refs[4:]
    m_rows = Q_PER_KV * tq
    rc = min(ATTN_ROW_CHUNK, m_rows)

    for g in range(Q_PER_KV):
        qs_ref[g * tq:(g + 1) * tq, :] = q_ref[:, g * HEAD_DIM:(g + 1) * HEAD_DIM]

    def scores(kblk):
        return lax.dot_general(qs_ref[...], kblk, (((1,), (1,)), ((), ())),
                               preferred_element_type=F32)

    def softmax_pv(s_ref, p_ref, vblk, ncols, first):
        for r0 in range(0, m_rows, rc):
            rows = slice(r0, r0 + rc)
            cols = [s_ref[rows, c * LANES:(c + 1) * LANES] for c in range(ncols)]
            mx = cols[0]
            for c in cols[1:]:
                mx = jnp.maximum(mx, c)
            m_cur = jnp.max(mx, axis=-1, keepdims=True)
            if first:
                m_new = jnp.broadcast_to(m_cur, (rc, LANES))
            else:
                m_prev = m_ref[rows, :]
                m_new = jnp.maximum(m_prev, m_cur)
            ps = [jnp.exp2(c - m_new) for c in cols]
            psum = ps[0]
            for p in ps[1:]:
                psum = psum + p
            for c in range(ncols):
                p_ref[rows, c * LANES:(c + 1) * LANES] = ps[c].astype(BF16)
            if first:
                l_ref[rows, :] = psum
            else:
                alpha = jnp.exp2(m_prev - m_new)
                l_ref[rows, :] = alpha * l_ref[rows, :] + psum
                acc_ref[rows, :] = alpha * acc_ref[rows, :]
            m_ref[rows, :] = m_new
        pv = jnp.dot(p_ref[:, :ncols * LANES], vblk, preferred_element_type=F32)
        if first:
            acc_ref[...] = pv
        else:
            acc_ref[...] += pv

    def kblock(ref, i):
        return ref[pl.ds(pl.multiple_of(i * bk, bk), bk), :]

    if n_main:
        s0_ref[...] = scores(k_ref[0:bk, :])
    s1_ref[:, :bkc] = scores(kc_ref[...])
    softmax_pv(s1_ref, p1_ref, vc_ref[...], bkc // LANES, True)

    if n_main:
        ncols = bk // LANES
        s_bufs, p_bufs = (s0_ref, s1_ref), (p0_ref, p1_ref)
        unroll = min(ATTN_UNROLL, n_main)
        assert unroll % 2 == 0 and n_main % unroll == 0

        def group(j, produce_last):
            for u in range(unroll):
                b = unroll * j + u
                if u + 1 < unroll or produce_last:
                    s_bufs[(u + 1) % 2][...] = scores(kblock(k_ref, b + 1))
                softmax_pv(s_bufs[u % 2], p_bufs[u % 2], kblock(v_ref, b), ncols, False)

        def body(j, carry):
            group(j, True)
            return carry

        lax.fori_loop(0, n_main // unroll - 1, body, 0)
        group(n_main // unroll - 1, False)

    l_row = jnp.sum(l_ref[...], axis=-1, keepdims=True)
    o = acc_ref[...] / l_row
    for g in range(Q_PER_KV):
        o_ref[:, g * HEAD_DIM:(g + 1) * HEAD_DIM] = o[g * tq:(g + 1) * tq, :].astype(BF16)


def _attention(zq, ctx_kv, main_kv, *, tq):
    t = zq.shape[0]
    m_rows = Q_PER_KV * tq
    kv_spec = lambda length, col: pl.BlockSpec((length, HEAD_DIM), lambda h, i: (0, col + h))
    zc, kcol_c, vcol_c = ctx_kv
    bkc = zc.shape[0]
    assert bkc % LANES == 0
    args = [zq, zc, zc]
    in_specs = [pl.BlockSpec((tq, Q_PER_KV * HEAD_DIM), lambda h, i: (i, h)),
                kv_spec(bkc, kcol_c), kv_spec(bkc, vcol_c)]
    scratch = [pltpu.VMEM((m_rows, HEAD_DIM), BF16)]
    if main_kv is not None:
        zm, kcol_m, vcol_m = main_kv
        bk = ATTN_BK
        n_main = zm.shape[0] // bk
        assert n_main * bk == zm.shape[0] and bkc <= bk
        args += [zm, zm]
        in_specs += [kv_spec(zm.shape[0], kcol_m), kv_spec(zm.shape[0], vcol_m)]
        scratch += [pltpu.VMEM((m_rows, bk), F32), pltpu.VMEM((m_rows, bk), F32),
                    pltpu.VMEM((m_rows, bk), BF16), pltpu.VMEM((m_rows, bk), BF16)]
    else:
        bk, n_main = bkc, 0
        scratch += [pltpu.VMEM((m_rows, bkc), F32), pltpu.VMEM((m_rows, bkc), BF16)]
    scratch += [pltpu.VMEM((m_rows, LANES), F32), pltpu.VMEM((m_rows, LANES), F32),
                pltpu.VMEM((m_rows, HEAD_DIM), F32)]
    return pl.pallas_call(
        functools.partial(_attn_kernel, tq=tq, bk=bk, n_main=n_main, bkc=bkc),
        grid=(N_KV_HEADS, t // tq),
        in_specs=in_specs,
        out_specs=pl.BlockSpec((tq, Q_PER_KV * HEAD_DIM), lambda h, i: (i, h)),
        out_shape=jax.ShapeDtypeStruct((t, ATTN_WIDTH), BF16),
        scratch_shapes=scratch,
        compiler_params=_cparams(("arbitrary", "arbitrary")),
        name="attn",
    )(*args)


def _dft_tables(t):
    n2 = min(t, 128) if t > 256 else t
    n1 = t // n2
    assert n1 * n2 == t

    def cs(n):
        idx = np.arange(n)
        ang = 2.0 * np.pi * ((idx[:, None] * idx[None, :]) % n) / n
        return np.cos(ang), np.sin(ang)

    tabs = {"n1": n1, "n2": n2}
    c2, s2 = cs(n2)
    if n1 > 1:
        c1, s1 = cs(n1)
        tabs["m1"] = jnp.asarray(np.concatenate([c1, -s1], axis=0), F32)
        ang = 2.0 * np.pi * (np.arange(n1)[:, None] * np.arange(n2)[None, :]) / t
        tabs["twr"] = jnp.asarray(np.repeat(np.cos(ang), LANES, axis=1), F32)
        tabs["twi"] = jnp.asarray(np.repeat(-np.sin(ang), LANES, axis=1), F32)
        m2 = np.block([[c2, s2], [-s2, c2]])
    else:
        m2 = np.concatenate([c2, -s2], axis=0)
    tabs["m2"] = jnp.asarray(m2, F32)
    cc, sc = cs(FOURIER_GROUP_CH)
    tabs["mc"] = jnp.asarray(np.concatenate([cc, sc], axis=0), F32)
    tabs["norm"] = 1.0 / math.sqrt(t * FOURIER_GROUP_CH)
    return tabs


def _dft_a_kernel(f_ref, m1_ref, twr_ref, twi_ref, xr_ref, xi_ref, *, n1, nt2):
    w = FOURIER_WIDTH
    for j in range(nt2):
        x1 = jnp.dot(m1_ref[...], f_ref[:, j * w:(j + 1) * w], preferred_element_type=F32)
        wr = twr_ref[:, j * LANES:(j + 1) * LANES]
        wi = twi_ref[:, j * LANES:(j + 1) * LANES]
        for g in range(w // LANES):
            a = x1[:n1, g * LANES:(g + 1) * LANES]
            b = x1[n1:, g * LANES:(g + 1) * LANES]
            sl = slice(j * w + g * LANES, j * w + (g + 1) * LANES)
            xr_ref[:, sl] = (a * wr - b * wi).astype(BF16)
            xi_ref[:, sl] = (a * wi + b * wr).astype(BF16)


def _dft_b_kernel(*refs, n2, kb, complex_in, norm):
    if complex_in:
        xr_ref, xi_ref, m2_ref, mc_ref, o_ref = refs
    else:
        xr_ref, m2_ref, mc_ref, o_ref = refs
    c = FOURIER_GROUP_CH
    for j in range(kb):
        if complex_in:
            rhs = jnp.concatenate([xr_ref[j], xi_ref[j]], axis=0)
        else:
            rhs = xr_ref[j]
        p = jnp.dot(m2_ref[...], rhs, preferred_element_type=F32)
        for g in range(FOURIER_GROUPS):
            pg = jnp.concatenate([p[:n2, g * c:(g + 1) * c], p[n2:, g * c:(g + 1) * c]], axis=1)
            y = jnp.dot(pg.astype(BF16), mc_ref[...], preferred_element_type=F32) * norm
            o_ref[:, j, g * c:(g + 1) * c] = y.astype(BF16)


def _fourier(z, tabs):
    t = z.shape[0]
    n1, n2, w = tabs["n1"], tabs["n2"], FOURIER_WIDTH
    if n1 > 1:
        f = z[:, CB_FOURIER * COL_BLOCK:(CB_FOURIER + 1) * COL_BLOCK].reshape(n1, n2 * w)
        nt2 = 8
        assert n2 % nt2 == 0
        xr, xi = pl.pallas_call(
            functools.partial(_dft_a_kernel, n1=n1, nt2=nt2),
            grid=(n2 // nt2,),
            in_specs=[
                pl.BlockSpec((n1, nt2 * w), lambda i: (0, i)),
                pl.BlockSpec((2 * n1, n1), lambda i: (0, 0)),
                pl.BlockSpec((n1, nt2 * LANES), lambda i: (0, i)),
                pl.BlockSpec((n1, nt2 * LANES), lambda i: (0, i)),
            ],
            out_specs=[pl.BlockSpec((n1, nt2 * w), lambda i: (0, i))] * 2,
            out_shape=[jax.ShapeDtypeStruct((n1, n2 * w), BF16)] * 2,
            compiler_params=_cparams(("arbitrary",)),
            name="dft_a",
        )(f, tabs["m1"].astype(BF16), tabs["twr"], tabs["twi"])
        kb = 8
        assert n1 % kb == 0
        xs = [xr.reshape(n1, n2, w), xi.reshape(n1, n2, w)]
        x_specs = [pl.BlockSpec((kb, n2, w), lambda i: (i, 0, 0))] * 2
        complex_in = True
    else:
        kb = 1
        xs = [z.reshape(1, t, z.shape[1])]
        x_specs = [pl.BlockSpec((1, n2, w), lambda i: (0, 0, CB_FOURIER))]
        complex_in = False
    m2, mc = tabs["m2"].astype(BF16), tabs["mc"].astype(BF16)
    out = pl.pallas_call(
        functools.partial(_dft_b_kernel, n2=n2, kb=kb, complex_in=complex_in, norm=tabs["norm"]),
        grid=(n1 // kb,),
        in_specs=x_specs + [
            pl.BlockSpec(m2.shape, lambda i: (0, 0)),
            pl.BlockSpec(mc.shape, lambda i: (0, 0)),
        ],
        out_specs=pl.BlockSpec((n2, kb, w), lambda i: (0, i, 0)),
        out_shape=jax.ShapeDtypeStruct((n2, n1, w), BF16),
        compiler_params=_cparams(("arbitrary",)),
        name="dft_b",
    )(*xs, m2, mc)
    return out.reshape(t, w)


def _merge_kernel(a_ref, u_ref, vn_ref, fo_ref, g_ref, x_ref, wa_ref, ws_ref, wf_ref, wo_ref,
                  wsp_ref, bsp_ref, gate_ref, lg_ref, lb_ref, o_ref, gout_ref, *, tm):
    group_ch = SGU_WIDTH // SGU_GROUPS
    lane = lax.broadcasted_iota(jnp.int32, (SGU_CHUNK, LANES), 1)
    for ci in range(tm // SGU_CHUNK):
        rows = slice(ci * SGU_CHUNK, (ci + 1) * SGU_CHUNK)
        for lb in range(SGU_WIDTH // LANES):
            cols = slice(lb * LANES, (lb + 1) * LANES)
            y = jnp.dot(wsp_ref[lb], vn_ref[rows, cols], preferred_element_type=F32)
            mixed = jnp.where(lane < group_ch, y[:SGU_CHUNK], y[SGU_CHUNK:]) + bsp_ref[:, cols]
            gout_ref[rows, cols] = (u_ref[rows, cols].astype(F32) * mixed).astype(BF16)

    d = D_MODEL
    merged = g_ref[:, 0:d].astype(F32) * jnp.dot(a_ref[...], wa_ref[...], preferred_element_type=F32)
    merged += g_ref[:, d:2 * d].astype(F32) * jnp.dot(gout_ref[...], ws_ref[...],
                                                      preferred_element_type=F32)
    merged += g_ref[:, 2 * d:3 * d].astype(F32) * jnp.dot(fo_ref[...], wf_ref[...],
                                                          preferred_element_type=F32)
    mix = jnp.dot(merged.astype(BF16), wo_ref[...], preferred_element_type=F32)
    y = DEEPNORM_ALPHA * x_ref[...] + gate_ref[...] * mix
    o_ref[...] = _ln(y) * lg_ref[...] + lb_ref[...]


def _merge(a, z, f_out, x, w_br_attn, w_br_sgu, w_br_fourier, w_out, wsp2, bsp, gate1, ln_g, ln_b, *, tm):
    t, d = x.shape
    full = lambda arr: pl.BlockSpec(arr.shape, lambda i: (0,) * arr.ndim)
    vec = pl.BlockSpec((1, d), lambda i: (0, 0))
    return pl.pallas_call(
        functools.partial(_merge_kernel, tm=tm),
        grid=(t // tm,),
        in_specs=[
            pl.BlockSpec((tm, ATTN_WIDTH), lambda i: (i, 0)),
            pl.BlockSpec((tm, COL_BLOCK), lambda i: (i, CB_SGU_U)),
            pl.BlockSpec((tm, COL_BLOCK), lambda i: (i, CB_SGU_V)),
            pl.BlockSpec((tm, FOURIER_WIDTH), lambda i: (i, 0)),
            pl.BlockSpec((tm, N_BRANCHES * d), lambda i: (i, CB_GATES * COL_BLOCK // (N_BRANCHES * d))),
            pl.BlockSpec((tm, d), lambda i: (i, 0)),
            full(w_br_attn), full(w_br_sgu), full(w_br_fourier), full(w_out),
            full(wsp2), full(bsp), vec, vec, vec,
        ],
        out_specs=pl.BlockSpec((tm, d), lambda i: (i, 0)),
        out_shape=jax.ShapeDtypeStruct((t, d), F32),
        scratch_shapes=[pltpu.VMEM((tm, SGU_WIDTH), BF16)],
        compiler_params=_cparams(("arbitrary",)),
        name="merge",
    )(a, z, z, f_out, z, x, w_br_attn, w_br_sgu, w_br_fourier, w_out, wsp2, bsp, gate1, ln_g, ln_b)


def _ffn_kernel(x_ref, sh_ref, sc_ref, gate_ref, wup_ref, wd_ref, lg_ref, lb_ref, o_ref, *, hidden):
    x = x_ref[...]
    h = (_ln(x) * (1.0 + sc_ref[...]) + sh_ref[...]).astype(BF16)
    th = hidden // FFN_CHUNKS
    acc = None
    for c in range(FFN_CHUNKS):
        g = jnp.dot(h, wup_ref[:, c * th:(c + 1) * th], preferred_element_type=F32)
        u = jnp.dot(h, wup_ref[:, hidden + c * th:hidden + (c + 1) * th],
                    preferred_element_type=F32)
        act = (g * _sigmoid(g) * u).astype(BF16)
        part = jnp.dot(act, wd_ref[c * th:(c + 1) * th, :], preferred_element_type=F32)
        acc = part if acc is None else acc + part
    y = DEEPNORM_ALPHA * x + gate_ref[...] * acc
    o_ref[...] = _ln(y) * lg_ref[...] + lb_ref[...]


def _ffn(x, shift, scale, gate, w_up, w_down, ln_g, ln_b, *, tm):
    t, d = x.shape
    hidden = w_down.shape[0]
    th = hidden // FFN_CHUNKS
    assert th * FFN_CHUNKS == hidden and th % LANES == 0
    vec = pl.BlockSpec((1, d), lambda i: (0, 0))
    resident = lambda arr: pl.BlockSpec(arr.shape, lambda i: (0, 0), pipeline_mode=pl.Buffered(1))
    return pl.pallas_call(
        functools.partial(_ffn_kernel, hidden=hidden),
        grid=(t // tm,),
        in_specs=[
            pl.BlockSpec((tm, d), lambda i: (i, 0)),
            vec, vec, vec,
            resident(w_up), resident(w_down),
            vec, vec,
        ],
        out_specs=pl.BlockSpec((tm, d), lambda i: (i, 0)),
        out_shape=jax.ShapeDtypeStruct((t, d), F32),
        compiler_params=_cparams(("arbitrary",)),
        name="ffn",
    )(x, shift, scale, gate, w_up, w_down, ln_g, ln_b)


def _rope_tables(n_tokens):
    rows = n_tokens // GRID_W
    pos_r = jnp.repeat(jnp.arange(rows, dtype=F32), GRID_W)
    pos_c = jnp.tile(jnp.arange(GRID_W, dtype=F32), rows)
    inv = ROPE_THETA ** (-jnp.arange(0, ROPE_AXIS_DIM, 2, dtype=F32) / ROPE_AXIS_DIM)
    ang = jnp.concatenate([pos_r[:, None] * inv, pos_c[:, None] * inv], axis=-1)
    cos, sin = jnp.cos(ang), jnp.sin(ang)
    cos2 = jnp.repeat(cos, 2, axis=-1)
    sin2 = jnp.stack([-sin, sin], axis=-1).reshape(n_tokens, HEAD_DIM)
    return cos2, sin2


def kernel(x, c, ctx, c_ctx, w_ada, b_ada, w_in, q_gain, k_gain, sgu_ln_g, sgu_ln_b, w_spatial,
           b_spatial, w_br_attn, w_br_sgu, w_br_fourier, w_out, ln1_g, ln1_b, w_up, w_down,
           ln2_g, ln2_b):
    batch, seq, d = x.shape
    n_ctx = ctx.shape[1]
    depth = w_in.shape[0]
    assert batch == 1 and d == D_MODEL and depth == DEPTH
    xl = x[0]
    xc = ctx[0]

    rope = _rope_tables(seq)
    tabs_x = _dft_tables(seq)
    tabs_c = _dft_tables(n_ctx)

    c2 = jnp.stack([c[0], c_ctx], axis=1)
    mod = _ada(c2, w_ada, b_ada)

    tm_x = _row_tile(seq, 512)
    tm_c = _row_tile(n_ctx, 256)
    tm_merge_x = _row_tile(seq, 512)
    tq_x = _row_tile(seq, 256)
    tq_c = _row_tile(n_ctx, 256)

    row = lambda v: v.reshape(1, -1)
    kcol = CB_KV * COL_BLOCK // HEAD_DIM
    vcol = kcol + N_KV_HEADS

    for l in range(depth):
        last = l == depth - 1
        mx = [mod[l, 0:1, i * d:(i + 1) * d] for i in range(6)]
        mc = [mod[l, 1:2, i * d:(i + 1) * d] for i in range(6)]
        w_in_l = w_in[l].astype(BF16)
        qg, kg = row(q_gain[l]), row(k_gain[l])
        sg, sb = row(sgu_ln_g[l]), row(sgu_ln_b[l])
        wsp2 = w_spatial[l].astype(BF16).reshape(SGU_GROUPS // 2, 2 * SGU_CHUNK, SGU_CHUNK)
        bsp = jnp.repeat(b_spatial[l].T, SGU_WIDTH // SGU_GROUPS, axis=1)
        mixer_w = (w_br_attn[l].astype(BF16), w_br_sgu[l].astype(BF16), w_br_fourier[l].astype(BF16),
                   w_out[l].astype(BF16), wsp2, bsp)
        w_up_l, w_down_l = w_up[l].astype(BF16), w_down[l].astype(BF16)
        l1g, l1b, l2g, l2b = row(ln1_g[l]), row(ln1_b[l]), row(ln2_g[l]), row(ln2_b[l])

        if last:
            zc = _in_proj(xc, mc[0], mc[1], w_in_l, qg, kg, sg, sb, None, cb0=CB_KV, ncb=1, tm=tm_c)
            ctx_kv = (zc, 0, N_KV_HEADS)
        else:
            zc = _in_proj(xc, mc[0], mc[1], w_in_l, qg, kg, sg, sb, None, cb0=0, ncb=N_COL_BLOCKS, tm=tm_c)
            ctx_kv = (zc, kcol, vcol)
            ac = _attention(zc, ctx_kv, None, tq=tq_c)
            fc = _fourier(zc, tabs_c)
            xc1 = _merge(ac, zc, fc, xc, *mixer_w, mc[2], l1g, l1b, tm=tm_c)
            xc = _ffn(xc1, mc[3], mc[4], mc[5], w_up_l, w_down_l, l2g, l2b, tm=tm_c)

        zx = _in_proj(xl, mx[0], mx[1], w_in_l, qg, kg, sg, sb, rope, cb0=0, ncb=N_COL_BLOCKS, tm=tm_x)
        ax = _attention(zx, ctx_kv, (zx, kcol, vcol), tq=tq_x)
        fx = _fourier(zx, tabs_x)
        xl1 = _merge(ax, zx, fx, xl, *mixer_w, mx[2], l1g, l1b, tm=tm_merge_x)
        xl = _ffn(xl1, mx[3], mx[4], mx[5], w_up_l, w_down_l, l2g, l2b, tm=tm_merge_x)

    return xl[None]
```

```python
import functools
import math

import jax
import jax.numpy as jnp
import numpy as np
from jax import lax
from jax.experimental import pallas as pl
from jax.experimental.pallas import tpu as pltpu

F32 = jnp.float32
BF16 = jnp.bfloat16

D_MODEL = 1024
DEPTH = 2
GRID_W = 64
N_Q_HEADS = 8
N_KV_HEADS = 2
HEAD_DIM = 128
Q_PER_KV = N_Q_HEADS // N_KV_HEADS
ATTN_WIDTH = N_Q_HEADS * HEAD_DIM
KV_WIDTH = N_KV_HEADS * HEAD_DIM
ROPE_AXIS_DIM = HEAD_DIM // 2
ROPE_THETA = 10000.0
SGU_GROUPS = 8
SGU_WIDTH = 512
SGU_CHUNK = 128
FOURIER_GROUPS = 4
FOURIER_WIDTH = 512
FOURIER_GROUP_CH = FOURIER_WIDTH // FOURIER_GROUPS
N_BRANCHES = 3
FFN_HIDDEN = -(-8 * D_MODEL // (3 * 256)) * 256
IN_WIDTH = ATTN_WIDTH + 2 * KV_WIDTH + 2 * SGU_WIDTH + FOURIER_WIDTH + N_BRANCHES * D_MODEL
DEEPNORM_ALPHA = (2 * DEPTH) ** 0.25
LN_EPS = 1e-6
RMS_EPS = 1e-6
ATTN_SCALE = HEAD_DIM ** -0.5
Q_PRESCALE = ATTN_SCALE * math.log2(math.e)

COL_BLOCK = 512
N_COL_BLOCKS = IN_WIDTH // COL_BLOCK
CB_Q0 = 0
CB_KV = 2
CB_SGU_U = 3
CB_SGU_V = 4
CB_FOURIER = 5
CB_GATES = 6

LANES = 128
VMEM_LIMIT_BYTES = 52 * 1024 * 1024

ADA_TN = 1024
ATTN_BK = 512
ATTN_UNROLL = 8
FFN_CHUNKS = 2


def _cparams(sem):
    return pltpu.CompilerParams(dimension_semantics=sem, vmem_limit_bytes=VMEM_LIMIT_BYTES)


def _row_tile(t, want):
    tm = min(t, want)
    assert t % tm == 0
    return tm


def _sigmoid(x):
    return 1.0 / (1.0 + jnp.exp(-x))


def _gelu_tanh(x):
    c = math.sqrt(2.0 / math.pi)
    return 0.5 * x * (1.0 + jnp.tanh(c * (x + 0.044715 * (x * x * x))))


def _ln(x):
    mu = jnp.mean(x, axis=-1, keepdims=True)
    xc = x - mu
    var = jnp.mean(xc * xc, axis=-1, keepdims=True)
    return xc * lax.rsqrt(var + LN_EPS)


def _ada_kernel(c_ref, w_ref, b_ref, o_ref):
    c = c_ref[...]
    s = c * _sigmoid(c)
    w = w_ref[0]
    b = b_ref[0]
    r0 = jnp.sum(w * s[:, 0:1], axis=0, keepdims=True) + b
    r1 = jnp.sum(w * s[:, 1:2], axis=0, keepdims=True) + b
    o_ref[0] = jnp.concatenate([r0, r1], axis=0)


def _ada(c2, w_ada, b_ada):
    depth, d, n = w_ada.shape
    tn = ADA_TN
    return pl.pallas_call(
        _ada_kernel,
        grid=(depth, n // tn),
        in_specs=[
            pl.BlockSpec((d, 2), lambda l, j: (0, 0)),
            pl.BlockSpec((1, d, tn), lambda l, j: (l, 0, j)),
            pl.BlockSpec((1, 1, tn), lambda l, j: (l, 0, j)),
        ],
        out_specs=pl.BlockSpec((1, 2, tn), lambda l, j: (l, 0, j)),
        out_shape=jax.ShapeDtypeStruct((depth, 2, n), F32),
        compiler_params=_cparams(("arbitrary", "arbitrary")),
        name="ada",
    )(c2, w_ada, b_ada.reshape(depth, 1, n))


def _rms_rope(zh, gain, cos, sin_signed, use_rope):
    y = zh * lax.rsqrt(jnp.mean(zh * zh, axis=-1, keepdims=True) + RMS_EPS) * gain
    if not use_rope:
        return y
    lane = lax.broadcasted_iota(jnp.int32, y.shape, 1)
    nxt = pltpu.roll(y, HEAD_DIM - 1, axis=1)
    prv = pltpu.roll(y, 1, axis=1)
    partner = jnp.where((lane & 1) == 0, nxt, prv)
    return y * cos + partner * sin_signed


def _in_kernel(x_ref, sh_ref, sc_ref, w_ref, qg_ref, kg_ref, lg_ref, lb_ref, cos_ref, sin_ref,
               o_ref, *, cb0, ncb, use_rope):
    h = (_ln(x_ref[...]) * (1.0 + sc_ref[...]) + sh_ref[...]).astype(BF16)
    cos = cos_ref[...] if use_rope else None
    sin = sin_ref[...] if use_rope else None

    def qk_heads(z, base, count, gain):
        for hh in range(count):
            sl = slice(hh * HEAD_DIM, (hh + 1) * HEAD_DIM)
            o_ref[:, base + hh * HEAD_DIM:base + (hh + 1) * HEAD_DIM] = _rms_rope(
                z[:, sl], gain, cos, sin, use_rope).astype(BF16)

    for jj in range(ncb):
        cb = cb0 + jj
        base = jj * COL_BLOCK
        cols = slice(base, base + COL_BLOCK)
        z = jnp.dot(h, w_ref[:, cols], preferred_element_type=F32)
        if cb < CB_KV:
            qk_heads(z, base, COL_BLOCK // HEAD_DIM, qg_ref[...] * Q_PRESCALE)
        elif cb == CB_KV:
            qk_heads(z, base, N_KV_HEADS, kg_ref[...])
            o_ref[:, base + KV_WIDTH:base + COL_BLOCK] = z[:, KV_WIDTH:].astype(BF16)
        elif cb == CB_SGU_U:
            o_ref[:, cols] = _gelu_tanh(z).astype(BF16)
        elif cb == CB_SGU_V:
            o_ref[:, cols] = (_ln(_gelu_tanh(z)) * lg_ref[...] + lb_ref[...]).astype(BF16)
        elif cb == CB_FOURIER:
            o_ref[:, cols] = z.astype(BF16)
        else:
            o_ref[:, cols] = _sigmoid(z).astype(BF16)


def _in_proj(x, shift, scale, w_in, q_gain, k_gain, sgu_g, sgu_b, rope, *, cb0, ncb, tm):
    t, d = x.shape
    assert cb0 % ncb == 0
    width = ncb * COL_BLOCK
    use_rope = rope is not None
    if use_rope:
        cos, sin = rope
    else:
        cos = sin = jnp.zeros((8, HEAD_DIM), F32)
    rope_spec = (pl.BlockSpec((tm, HEAD_DIM), lambda i: (i, 0)) if use_rope
                 else pl.BlockSpec((8, HEAD_DIM), lambda i: (0, 0)))
    vec = lambda n: pl.BlockSpec((1, n), lambda i: (0, 0))
    return pl.pallas_call(
        functools.partial(_in_kernel, cb0=cb0, ncb=ncb, use_rope=use_rope),
        grid=(t // tm,),
        in_specs=[
            pl.BlockSpec((tm, d), lambda i: (i, 0)),
            vec(d), vec(d),
            pl.BlockSpec((d, width), lambda i: (0, cb0 // ncb), pipeline_mode=pl.Buffered(1)),
            vec(HEAD_DIM), vec(HEAD_DIM), vec(SGU_WIDTH), vec(SGU_WIDTH),
            rope_spec, rope_spec,
        ],
        out_specs=pl.BlockSpec((tm, width), lambda i: (i, 0)),
        out_shape=jax.ShapeDtypeStruct((t, width), BF16),
        compiler_params=_cparams(("arbitrary",)),
        name="in_proj",
    )(x, shift, scale, w_in, q_gain, k_gain, sgu_g, sgu_b, cos, sin)


def _attn_kernel(*refs, tq, bk, n_main, bkc):
    if n_main:
        q_ref, kc_ref, vc_ref, k_ref, v_ref, o_ref = refs[:6]
        qs_ref, s0_ref, s1_ref, p0_ref, p1_ref, m_ref, l_ref, acc_ref = refs[6:]
    else:
        q_ref, kc_ref, vc_ref, o_ref = refs[:4]
        qs_ref, s0_ref, p0_ref, m_ref, l_ref, acc_ref = refs[4:]
    ncols_c = bkc // LANES

    for g in range(Q_PER_KV):
        qs_ref[g * tq:(g + 1) * tq, :] = q_ref[:, g * HEAD_DIM:(g + 1) * HEAD_DIM]
    m_ref[...] = jnp.full(m_ref.shape, -jnp.inf, F32)
    l_ref[...] = jnp.zeros(l_ref.shape, F32)
    acc_ref[...] = jnp.zeros(acc_ref.shape, F32)

    def scores(kblk):
        return lax.dot_general(qs_ref[...], kblk, (((1,), (1,)), ((), ())),
                               preferred_element_type=F32)

    def softmax_pv(s_ref, p_ref, vblk, ncols):
        cols = [s_ref[:, c * LANES:(c + 1) * LANES] for c in range(ncols)]
        mx = cols[0]
        for c in cols[1:]:
            mx = jnp.maximum(mx, c)
        m_prev = m_ref[...]
        m_new = jnp.maximum(m_prev, jnp.max(mx, axis=-1, keepdims=True))
        ps = [jnp.exp2(c - m_new) for c in cols]
        psum = ps[0]
        for p in ps[1:]:
            psum = psum + p
        for c in range(ncols):
            p_ref[:, c * LANES:(c + 1) * LANES] = ps[c].astype(BF16)
        alpha = jnp.exp2(m_prev - m_new)
        l_ref[...] = alpha * l_ref[...] + psum
        m_ref[...] = m_new
        acc_ref[...] = alpha * acc_ref[...] + jnp.dot(p_ref[:, :ncols * LANES], vblk,
                                                      preferred_element_type=F32)

    def kblock(ref, i):
        return ref[pl.ds(pl.multiple_of(i * bk, bk), bk), :]

    if n_main:
        ncols = bk // LANES
        s_bufs, p_bufs = (s0_ref, s1_ref), (p0_ref, p1_ref)
        unroll = min(ATTN_UNROLL, n_main)
        assert unroll % 2 == 0 and n_main % unroll == 0
        s0_ref[...] = scores(k_ref[0:bk, :])

        def group(j, last):
            for u in range(unroll):
                b = unroll * j + u
                if u + 1 < unroll or not last:
                    s_bufs[(u + 1) % 2][...] = scores(kblock(k_ref, b + 1))
                else:
                    s0_ref[:, :bkc] = scores(kc_ref[...])
                softmax_pv(s_bufs[u % 2], p_bufs[u % 2], kblock(v_ref, b), ncols)

        def body(j, carry):
            group(j, False)
            return carry

        lax.fori_loop(0, n_main // unroll - 1, body, 0)
        group(n_main // unroll - 1, True)
    else:
        s0_ref[...] = scores(kc_ref[...])
    softmax_pv(s0_ref, p0_ref, vc_ref[...], ncols_c)

    l_row = jnp.sum(l_ref[...], axis=-1, keepdims=True)
    o = acc_ref[...] / l_row
    for g in range(Q_PER_KV):
        o_ref[:, g * HEAD_DIM:(g + 1) * HEAD_DIM] = o[g * tq:(g + 1) * tq, :].astype(BF16)


def _attention(zq, ctx_kv, main_kv, *, tq):
    t = zq.shape[0]
    m_rows = Q_PER_KV * tq
    kv_spec = lambda length, col: pl.BlockSpec((length, HEAD_DIM), lambda h, i: (0, col + h))
    zc, kcol_c, vcol_c = ctx_kv
    bkc = zc.shape[0]
    assert bkc % LANES == 0
    args = [zq, zc, zc]
    in_specs = [pl.BlockSpec((tq, Q_PER_KV * HEAD_DIM), lambda h, i: (i, h)),
                kv_spec(bkc, kcol_c), kv_spec(bkc, vcol_c)]
    scratch = [pltpu.VMEM((m_rows, HEAD_DIM), BF16)]
    if main_kv is not None:
        zm, kcol_m, vcol_m = main_kv
        bk = ATTN_BK
        n_main = zm.shape[0] // bk
        assert n_main * bk == zm.shape[0] and bkc <= bk
        args += [zm, zm]
        in_specs += [kv_spec(zm.shape[0], kcol_m), kv_spec(zm.shape[0], vcol_m)]
        scratch += [pltpu.VMEM((m_rows, bk), F32), pltpu.VMEM((m_rows, bk), F32),
                    pltpu.VMEM((m_rows, bk), BF16), pltpu.VMEM((m_rows, bk), BF16)]
    else:
        bk, n_main = bkc, 0
        scratch += [pltpu.VMEM((m_rows, bkc), F32), pltpu.VMEM((m_rows, bkc), BF16)]
    scratch += [pltpu.VMEM((m_rows, LANES), F32), pltpu.VMEM((m_rows, LANES), F32),
                pltpu.VMEM((m_rows, HEAD_DIM), F32)]
    return pl.pallas_call(
        functools.partial(_attn_kernel, tq=tq, bk=bk, n_main=n_main, bkc=bkc),
        grid=(N_KV_HEADS, t // tq),
        in_specs=in_specs,
        out_specs=pl.BlockSpec((tq, Q_PER_KV * HEAD_DIM), lambda h, i: (i, h)),
        out_shape=jax.ShapeDtypeStruct((t, ATTN_WIDTH), BF16),
        scratch_shapes=scratch,
        compiler_params=_cparams(("arbitrary", "arbitrary")),
        name="attn",
    )(*args)


def _dft_tables(t):
    n2 = min(t, 128) if t > 256 else t
    n1 = t // n2
    assert n1 * n2 == t

    def cs(n):
        idx = np.arange(n)
        ang = 2.0 * np.pi * ((idx[:, None] * idx[None, :]) % n) / n
        return np.cos(ang), np.sin(ang)

    tabs = {"n1": n1, "n2": n2}
    c2, s2 = cs(n2)
    if n1 > 1:
        c1, s1 = cs(n1)
        tabs["m1"] = jnp.asarray(np.concatenate([c1, -s1], axis=0), F32)
        ang = 2.0 * np.pi * (np.arange(n1)[:, None] * np.arange(n2)[None, :]) / t
        tabs["twr"] = jnp.asarray(np.repeat(np.cos(ang), LANES, axis=1), F32)
        tabs["twi"] = jnp.asarray(np.repeat(-np.sin(ang), LANES, axis=1), F32)
        m2 = np.block([[c2, s2], [-s2, c2]])
    else:
        m2 = np.concatenate([c2, -s2], axis=0)
    tabs["m2"] = jnp.asarray(m2, F32)
    cc, sc = cs(FOURIER_GROUP_CH)
    tabs["mc"] = jnp.asarray(np.concatenate([cc, sc], axis=0), F32)
    tabs["norm"] = 1.0 / math.sqrt(t * FOURIER_GROUP_CH)
    return tabs


def _dft_a_kernel(f_ref, m1_ref, twr_ref, twi_ref, xr_ref, xi_ref, *, n1, nt2):
    w = FOURIER_WIDTH
    for j in range(nt2):
        x1 = jnp.dot(m1_ref[...], f_ref[:, j * w:(j + 1) * w], preferred_element_type=F32)
        wr = twr_ref[:, j * LANES:(j + 1) * LANES]
        wi = twi_ref[:, j * LANES:(j + 1) * LANES]
        for g in range(w // LANES):
            a = x1[:n1, g * LANES:(g + 1) * LANES]
            b = x1[n1:, g * LANES:(g + 1) * LANES]
            sl = slice(j * w + g * LANES, j * w + (g + 1) * LANES)
            xr_ref[:, sl] = (a * wr - b * wi).astype(BF16)
            xi_ref[:, sl] = (a * wi + b * wr).astype(BF16)


def _dft_b_kernel(*refs, n2, kb, complex_in, norm):
    if complex_in:
        xr_ref, xi_ref, m2_ref, mc_ref, o_ref = refs
    else:
        xr_ref, m2_ref, mc_ref, o_ref = refs
    c = FOURIER_GROUP_CH
    for j in range(kb):
        if complex_in:
            rhs = jnp.concatenate([xr_ref[j], xi_ref[j]], axis=0)
        else:
            rhs = xr_ref[j]
        p = jnp.dot(m2_ref[...], rhs, preferred_element_type=F32)
        for g in range(FOURIER_GROUPS):
            pg = jnp.concatenate([p[:n2, g * c:(g + 1) * c], p[n2:, g * c:(g + 1) * c]], axis=1)
            y = jnp.dot(pg.astype(BF16), mc_ref[...], preferred_element_type=F32) * norm
            o_ref[:, j, g * c:(g + 1) * c] = y


def _fourier(z, tabs):
    t = z.shape[0]
    n1, n2, w = tabs["n1"], tabs["n2"], FOURIER_WIDTH
    if n1 > 1:
        f = z[:, CB_FOURIER * COL_BLOCK:(CB_FOURIER + 1) * COL_BLOCK].reshape(n1, n2 * w)
        nt2 = 8
        assert n2 % nt2 == 0
        xr, xi = pl.pallas_call(
            functools.partial(_dft_a_kernel, n1=n1, nt2=nt2),
            grid=(n2 // nt2,),
            in_specs=[
                pl.BlockSpec((n1, nt2 * w), lambda i: (0, i)),
                pl.BlockSpec((2 * n1, n1), lambda i: (0, 0)),
                pl.BlockSpec((n1, nt2 * LANES), lambda i: (0, i)),
                pl.BlockSpec((n1, nt2 * LANES), lambda i: (0, i)),
            ],
            out_specs=[pl.BlockSpec((n1, nt2 * w), lambda i: (0, i))] * 2,
            out_shape=[jax.ShapeDtypeStruct((n1, n2 * w), BF16)] * 2,
            compiler_params=_cparams(("arbitrary",)),
            name="dft_a",
        )(f, tabs["m1"].astype(BF16), tabs["twr"], tabs["twi"])
        kb = 8
        assert n1 % kb == 0
        xs = [xr.reshape(n1, n2, w), xi.reshape(n1, n2, w)]
        x_specs = [pl.BlockSpec((kb, n2, w), lambda i: (i, 0, 0))] * 2
        complex_in = True
    else:
        kb = 1
        xs = [z.reshape(1, t, z.shape[1])]
        x_specs = [pl.BlockSpec((1, n2, w), lambda i: (0, 0, CB_FOURIER))]
        complex_in = False
    m2, mc = tabs["m2"].astype(BF16), tabs["mc"].astype(BF16)
    out = pl.pallas_call(
        functools.partial(_dft_b_kernel, n2=n2, kb=kb, complex_in=complex_in, norm=tabs["norm"]),
        grid=(n1 // kb,),
        in_specs=x_specs + [
            pl.BlockSpec(m2.shape, lambda i: (0, 0)),
            pl.BlockSpec(mc.shape, lambda i: (0, 0)),
        ],
        out_specs=pl.BlockSpec((n2, kb, w), lambda i: (0, i, 0)),
        out_shape=jax.ShapeDtypeStruct((n2, n1, w), F32),
        compiler_params=_cparams(("arbitrary",)),
        name="dft_b",
    )(*xs, m2, mc)
    return out.reshape(t, w)


def _merge_kernel(a_ref, u_ref, vn_ref, fo_ref, g_ref, x_ref, wa_ref, ws_ref, wf_ref, wo_ref,
                  wsp_ref, bsp_ref, gate_ref, lg_ref, lb_ref, o_ref, gout_ref, *, tm):
    group_ch = SGU_WIDTH // SGU_GROUPS
    lane = lax.broadcasted_iota(jnp.int32, (SGU_CHUNK, LANES), 1)
    for ci in range(tm // SGU_CHUNK):
        rows = slice(ci * SGU_CHUNK, (ci + 1) * SGU_CHUNK)
        for lb in range(SGU_WIDTH // LANES):
            cols = slice(lb * LANES, (lb + 1) * LANES)
            y = jnp.dot(wsp_ref[lb], vn_ref[rows, cols], preferred_element_type=F32)
            mixed = jnp.where(lane < group_ch, y[:SGU_CHUNK], y[SGU_CHUNK:]) + bsp_ref[:, cols]
            gout_ref[rows, cols] = (u_ref[rows, cols].astype(F32) * mixed).astype(BF16)

    d = D_MODEL
    merged = g_ref[:, 0:d].astype(F32) * jnp.dot(a_ref[...], wa_ref[...], preferred_element_type=F32)
    merged += g_ref[:, d:2 * d].astype(F32) * jnp.dot(gout_ref[...], ws_ref[...],
                                                      preferred_element_type=F32)
    merged += g_ref[:, 2 * d:3 * d].astype(F32) * jnp.dot(fo_ref[...].astype(BF16), wf_ref[...],
                                                          preferred_element_type=F32)
    mix = jnp.dot(merged.astype(BF16), wo_ref[...], preferred_element_type=F32)
    y = DEEPNORM_ALPHA * x_ref[...] + gate_ref[...] * mix
    o_ref[...] = _ln(y) * lg_ref[...] + lb_ref[...]


def _merge(a, z, f_out, x, w_br_attn, w_br_sgu, w_br_fourier, w_out, wsp2, bsp, gate1, ln_g, ln_b, *, tm):
    t, d = x.shape
    full = lambda arr: pl.BlockSpec(arr.shape, lambda i: (0,) * arr.ndim)
    vec = pl.BlockSpec((1, d), lambda i: (0, 0))
    return pl.pallas_call(
        functools.partial(_merge_kernel, tm=tm),
        grid=(t // tm,),
        in_specs=[
            pl.BlockSpec((tm, ATTN_WIDTH), lambda i: (i, 0)),
            pl.BlockSpec((tm, COL_BLOCK), lambda i: (i, CB_SGU_U)),
            pl.BlockSpec((tm, COL_BLOCK), lambda i: (i, CB_SGU_V)),
            pl.BlockSpec((tm, FOURIER_WIDTH), lambda i: (i, 0)),
            pl.BlockSpec((tm, N_BRANCHES * d), lambda i: (i, CB_GATES * COL_BLOCK // (N_BRANCHES * d))),
            pl.BlockSpec((tm, d), lambda i: (i, 0)),
            full(w_br_attn), full(w_br_sgu), full(w_br_fourier), full(w_out),
            full(wsp2), full(bsp), vec, vec, vec,
        ],
        out_specs=pl.BlockSpec((tm, d), lambda i: (i, 0)),
        out_shape=jax.ShapeDtypeStruct((t, d), F32),
        scratch_shapes=[pltpu.VMEM((tm, SGU_WIDTH), BF16)],
        compiler_params=_cparams(("arbitrary",)),
        name="merge",
    )(a, z, z, f_out, z, x, w_br_attn, w_br_sgu, w_br_fourier, w_out, wsp2, bsp, gate1, ln_g, ln_b)


def _ffn_kernel(x_ref, sh_ref, sc_ref, gate_ref, wup_ref, wd_ref, lg_ref, lb_ref, o_ref, *, hidden):
    x = x_ref[...]
    h = (_ln(x) * (1.0 + sc_ref[...]) + sh_ref[...]).astype(BF16)
    th = hidden // FFN_CHUNKS
    acc = None
    for c in range(FFN_CHUNKS):
        g = jnp.dot(h, wup_ref[:, c * th:(c + 1) * th], preferred_element_type=F32)
        u = jnp.dot(h, wup_ref[:, hidden + c * th:hidden + (c + 1) * th],
                    preferred_element_type=F32)
        act = (g * _sigmoid(g) * u).astype(BF16)
        part = jnp.dot(act, wd_ref[c * th:(c + 1) * th, :], preferred_element_type=F32)
        acc = part if acc is None else acc + part
    y = DEEPNORM_ALPHA * x + gate_ref[...] * acc
    o_ref[...] = _ln(y) * lg_ref[...] + lb_ref[...]


def _ffn(x, shift, scale, gate, w_up, w_down, ln_g, ln_b, *, tm):
    t, d = x.shape
    hidden = w_down.shape[0]
    th = hidden // FFN_CHUNKS
    assert th * FFN_CHUNKS == hidden and th % LANES == 0
    vec = pl.BlockSpec((1, d), lambda i: (0, 0))
    resident = lambda arr: pl.BlockSpec(arr.shape, lambda i: (0, 0), pipeline_mode=pl.Buffered(1))
    return pl.pallas_call(
        functools.partial(_ffn_kernel, hidden=hidden),
        grid=(t // tm,),
        in_specs=[
            pl.BlockSpec((tm, d), lambda i: (i, 0)),
            vec, vec, vec,
            resident(w_up), resident(w_down),
            vec, vec,
        ],
        out_specs=pl.BlockSpec((tm, d), lambda i: (i, 0)),
        out_shape=jax.ShapeDtypeStruct((t, d), F32),
        compiler_params=_cparams(("arbitrary",)),
        name="ffn",
    )(x, shift, scale, gate, w_up, w_down, ln_g, ln_b)


def _rope_tables(n_tokens):
    rows = n_tokens // GRID_W
    pos_r = np.repeat(np.arange(rows, dtype=np.float64), GRID_W)
    pos_c = np.tile(np.arange(GRID_W, dtype=np.float64), rows)
    inv = ROPE_THETA ** (-np.arange(0, ROPE_AXIS_DIM, 2, dtype=np.float64) / ROPE_AXIS_DIM)
    ang = np.concatenate([pos_r[:, None] * inv, pos_c[:, None] * inv], axis=-1)
    cos, sin = np.cos(ang), np.sin(ang)
    cos2 = np.repeat(cos, 2, axis=-1)
    sin2 = np.stack([-sin, sin], axis=-1).reshape(n_tokens, HEAD_DIM)
    return jnp.asarray(cos2, F32), jnp.asarray(sin2, F32)


def kernel(x, c, ctx, c_ctx, w_ada, b_ada, w_in, q_gain, k_gain, sgu_ln_g, sgu_ln_b, w_spatial,
           b_spatial, w_br_attn, w_br_sgu, w_br_fourier, w_out, ln1_g, ln1_b, w_up, w_down,
           ln2_g, ln2_b):
    batch, seq, d = x.shape
    n_ctx = ctx.shape[1]
    depth = w_in.shape[0]
    assert batch == 1 and d == D_MODEL and depth == DEPTH
    xl = x[0]
    xc = ctx[0]

    rope = _rope_tables(seq)
    tabs_x = _dft_tables(seq)
    tabs_c = _dft_tables(n_ctx)

    c2 = jnp.stack([c[0], c_ctx], axis=1)
    mod = _ada(c2, w_ada, b_ada)

    tm_x = _row_tile(seq, 512)
    tm_c = _row_tile(n_ctx, 256)
    tm_merge_x = _row_tile(seq, 512)
    tq_x = _row_tile(seq, 256)
    tq_c = _row_tile(n_ctx, 256)

    row = lambda v: v.reshape(1, -1)
    kcol = CB_KV * COL_BLOCK // HEAD_DIM
    vcol = kcol + N_KV_HEADS

    for l in range(depth):
        last = l == depth - 1
        mx = [mod[l, 0:1, i * d:(i + 1) * d] for i in range(6)]
        mc = [mod[l, 1:2, i * d:(i + 1) * d] for i in range(6)]
        w_in_l = w_in[l].astype(BF16)
        qg, kg = row(q_gain[l]), row(k_gain[l])
        sg, sb = row(sgu_ln_g[l]), row(sgu_ln_b[l])
        wsp2 = w_spatial[l].astype(BF16).reshape(SGU_GROUPS // 2, 2 * SGU_CHUNK, SGU_CHUNK)
        bsp = jnp.repeat(b_spatial[l].T, SGU_WIDTH // SGU_GROUPS, axis=1)
        mixer_w = (w_br_attn[l].astype(BF16), w_br_sgu[l].astype(BF16), w_br_fourier[l].astype(BF16),
                   w_out[l].astype(BF16), wsp2, bsp)
        w_up_l, w_down_l = w_up[l].astype(BF16), w_down[l].astype(BF16)
        l1g, l1b, l2g, l2b = row(ln1_g[l]), row(ln1_b[l]), row(ln2_g[l]), row(ln2_b[l])

        if last:
            zc = _in_proj(xc, mc[0], mc[1], w_in_l, qg, kg, sg, sb, None, cb0=CB_KV, ncb=1, tm=tm_c)
            ctx_kv = (zc, 0, N_KV_HEADS)
        else:
            zc = _in_proj(xc, mc[0], mc[1], w_in_l, qg, kg, sg, sb, None, cb0=0, ncb=N_COL_BLOCKS, tm=tm_c)
            ctx_kv = (zc, kcol, vcol)
            ac = _attention(zc, ctx_kv, None, tq=tq_c)
            fc = _fourier(zc, tabs_c)
            xc1 = _merge(ac, zc, fc, xc, *mixer_w, mc[2], l1g, l1b, tm=tm_c)
            xc = _ffn(xc1, mc[3], mc[4], mc[5], w_up_l, w_down_l, l2g, l2b, tm=tm_c)

        zx = _in_proj(xl, mx[0], mx[1], w_in_l, qg, kg, sg, sb, rope, cb0=0, ncb=N_COL_BLOCKS, tm=tm_x)
        ax = _attention(zx, ctx_kv, (zx, kcol, vcol), tq=tq_x)
        fx = _fourier(zx, tabs_x)
        xl1 = _merge(ax, zx, fx, xl, *mixer_w, mx[2], l1g, l1b, tm=tm_merge_x)
        xl = _ffn(xl1, mx[3], mx[4], mx[5], w_up_l, w_down_l, l2g, l2b, tm=tm_merge_x)

    return xl[None]
```

```python
import functools
import math

import jax
import jax.numpy as jnp
import numpy as np
from jax import lax
from jax.experimental import pallas as pl
from jax.experimental.pallas import tpu as pltpu

F32 = jnp.float32
BF16 = jnp.bfloat16

D_MODEL = 1024
DEPTH = 2
GRID_W = 64
N_Q_HEADS = 8
N_KV_HEADS = 2
HEAD_DIM = 128
Q_PER_KV = N_Q_HEADS // N_KV_HEADS
ATTN_WIDTH = N_Q_HEADS * HEAD_DIM
KV_WIDTH = N_KV_HEADS * HEAD_DIM
ROPE_AXIS_DIM = HEAD_DIM // 2
ROPE_THETA = 10000.0
SGU_GROUPS = 8
SGU_WIDTH = 512
SGU_CHUNK = 128
FOURIER_GROUPS = 4
FOURIER_WIDTH = 512
FOURIER_GROUP_CH = FOURIER_WIDTH // FOURIER_GROUPS
N_BRANCHES = 3
FFN_HIDDEN = -(-8 * D_MODEL // (3 * 256)) * 256
IN_WIDTH = ATTN_WIDTH + 2 * KV_WIDTH + 2 * SGU_WIDTH + FOURIER_WIDTH + N_BRANCHES * D_MODEL
DEEPNORM_ALPHA = (2 * DEPTH) ** 0.25
LN_EPS = 1e-6
RMS_EPS = 1e-6
ATTN_SCALE = HEAD_DIM ** -0.5
Q_PRESCALE = ATTN_SCALE * math.log2(math.e)

COL_BLOCK = 512
N_COL_BLOCKS = IN_WIDTH // COL_BLOCK
CB_KV = 2
CB_SGU_U = 3
CB_SGU_V = 4
CB_FOURIER = 5
CB_GATES = 6

LANES = 128
V7X_VMEM_BYTES = 64 * 1024 * 1024
VMEM_LIMIT_BYTES = V7X_VMEM_BYTES - 12 * 1024 * 1024

ADA_TN = 1024
ROW_TILE = 512
CTX_ROW_TILE = 256
ATTN_TQ = 256
ATTN_BK = 512
ATTN_UNROLL = 8
FFN_CHUNKS = 2
DFT_POINTS = 128
DFT_A_T2_PER_STEP = 8
DFT_B_K1_PER_STEP = 8


def _cparams(sem):
    return pltpu.CompilerParams(dimension_semantics=sem, vmem_limit_bytes=VMEM_LIMIT_BYTES)


def _row_tile(t, want):
    tm = min(t, want)
    assert t % tm == 0
    return tm


def _sigmoid(x):
    return 1.0 / (1.0 + jnp.exp(-x))


def _gelu_tanh(x):
    c = math.sqrt(2.0 / math.pi)
    return 0.5 * x * (1.0 + jnp.tanh(c * (x + 0.044715 * (x * x * x))))


def _ln(x):
    mu = jnp.mean(x, axis=-1, keepdims=True)
    xc = x - mu
    var = jnp.mean(xc * xc, axis=-1, keepdims=True)
    return xc * lax.rsqrt(var + LN_EPS)


def _ada_kernel(c_ref, w_ref, b_ref, o_ref):
    c = c_ref[...]
    s = c * _sigmoid(c)
    w = w_ref[0]
    b = b_ref[0]
    r0 = jnp.sum(w * s[:, 0:1], axis=0, keepdims=True) + b
    r1 = jnp.sum(w * s[:, 1:2], axis=0, keepdims=True) + b
    o_ref[0] = jnp.concatenate([r0, r1], axis=0)


def _ada(c2, w_ada, b_ada):
    depth, d, n = w_ada.shape
    tn = ADA_TN
    return pl.pallas_call(
        _ada_kernel,
        grid=(depth, n // tn),
        in_specs=[
            pl.BlockSpec((d, 2), lambda l, j: (0, 0)),
            pl.BlockSpec((1, d, tn), lambda l, j: (l, 0, j)),
            pl.BlockSpec((1, 1, tn), lambda l, j: (l, 0, j)),
        ],
        out_specs=pl.BlockSpec((1, 2, tn), lambda l, j: (l, 0, j)),
        out_shape=jax.ShapeDtypeStruct((depth, 2, n), F32),
        compiler_params=_cparams(("arbitrary", "arbitrary")),
        name="ada",
    )(c2, w_ada, b_ada.reshape(depth, 1, n))


def _rms_rope(zh, gain, cos, sin_signed, use_rope):
    y = zh * lax.rsqrt(jnp.mean(zh * zh, axis=-1, keepdims=True) + RMS_EPS) * gain
    if not use_rope:
        return y
    lane = lax.broadcasted_iota(jnp.int32, y.shape, 1)
    nxt = pltpu.roll(y, HEAD_DIM - 1, axis=1)
    prv = pltpu.roll(y, 1, axis=1)
    partner = jnp.where((lane & 1) == 0, nxt, prv)
    return y * cos + partner * sin_signed


def _in_kernel(x_ref, sh_ref, sc_ref, w_ref, qg_ref, kg_ref, lg_ref, lb_ref, cos_ref, sin_ref,
               o_ref, *, cb0, ncb, use_rope):
    h = (_ln(x_ref[...]) * (1.0 + sc_ref[...]) + sh_ref[...]).astype(BF16)
    cos = cos_ref[...] if use_rope else None
    sin = sin_ref[...] if use_rope else None

    def qk_heads(z, base, count, gain):
        for hh in range(count):
            sl = slice(hh * HEAD_DIM, (hh + 1) * HEAD_DIM)
            o_ref[:, base + hh * HEAD_DIM:base + (hh + 1) * HEAD_DIM] = _rms_rope(
                z[:, sl], gain, cos, sin, use_rope).astype(BF16)

    for jj in range(ncb):
        cb = cb0 + jj
        base = jj * COL_BLOCK
        cols = slice(base, base + COL_BLOCK)
        z = jnp.dot(h, w_ref[:, cols], preferred_element_type=F32)
        if cb < CB_KV:
            qk_heads(z, base, COL_BLOCK // HEAD_DIM, qg_ref[...] * Q_PRESCALE)
        elif cb == CB_KV:
            qk_heads(z, base, N_KV_HEADS, kg_ref[...])
            o_ref[:, base + KV_WIDTH:base + COL_BLOCK] = z[:, KV_WIDTH:].astype(BF16)
        elif cb == CB_SGU_U:
            o_ref[:, cols] = _gelu_tanh(z).astype(BF16)
        elif cb == CB_SGU_V:
            o_ref[:, cols] = (_ln(_gelu_tanh(z)) * lg_ref[...] + lb_ref[...]).astype(BF16)
        elif cb == CB_FOURIER:
            o_ref[:, cols] = z.astype(BF16)
        else:
            o_ref[:, cols] = _sigmoid(z).astype(BF16)


def _in_proj(x, shift, scale, w_in, q_gain, k_gain, sgu_g, sgu_b, rope, *, cb0, ncb, tm):
    t, d = x.shape
    assert cb0 % ncb == 0
    width = ncb * COL_BLOCK
    use_rope = rope is not None
    if use_rope:
        cos, sin = rope
    else:
        cos = sin = jnp.zeros((8, HEAD_DIM), F32)
    rope_spec = (pl.BlockSpec((tm, HEAD_DIM), lambda i: (i, 0)) if use_rope
                 else pl.BlockSpec((8, HEAD_DIM), lambda i: (0, 0)))
    vec = lambda n: pl.BlockSpec((1, n), lambda i: (0, 0))
    return pl.pallas_call(
        functools.partial(_in_kernel, cb0=cb0, ncb=ncb, use_rope=use_rope),
        grid=(t // tm,),
        in_specs=[
            pl.BlockSpec((tm, d), lambda i: (i, 0)),
            vec(d), vec(d),
            pl.BlockSpec((d, width), lambda i: (0, cb0 // ncb), pipeline_mode=pl.Buffered(1)),
            vec(HEAD_DIM), vec(HEAD_DIM), vec(SGU_WIDTH), vec(SGU_WIDTH),
            rope_spec, rope_spec,
        ],
        out_specs=pl.BlockSpec((tm, width), lambda i: (i, 0)),
        out_shape=jax.ShapeDtypeStruct((t, width), BF16),
        compiler_params=_cparams(("arbitrary",)),
        name="in_proj",
    )(x, shift, scale, w_in, q_gain, k_gain, sgu_g, sgu_b, cos, sin)


def _attn_kernel(*refs, tq, bk, n_main, bkc):
    if n_main:
        q_ref, kc_ref, vc_ref, k_ref, v_ref, o_ref = refs[:6]
        qs_ref, s0_ref, s1_ref, p0_ref, p1_ref, m_ref, l_ref, acc_ref = refs[6:]
    else:
        q_ref, kc_ref, vc_ref, o_ref = refs[:4]
        qs_ref, s0_ref, p0_ref, m_ref, l_ref, acc_ref = refs[4:]
    ncols_c = bkc // LANES

    for g in range(Q_PER_KV):
        qs_ref[g * tq:(g + 1) * tq, :] = q_ref[:, g * HEAD_DIM:(g + 1) * HEAD_DIM]
    m_ref[...] = jnp.full(m_ref.shape, -jnp.inf, F32)
    l_ref[...] = jnp.zeros(l_ref.shape, F32)
    acc_ref[...] = jnp.zeros(acc_ref.shape, F32)

    def scores(kblk):
        return lax.dot_general(qs_ref[...], kblk, (((1,), (1,)), ((), ())),
                               preferred_element_type=F32)

    def softmax_pv(s_ref, p_ref, vblk, ncols):
        cols = [s_ref[:, c * LANES:(c + 1) * LANES] for c in range(ncols)]
        mx = cols[0]
        for c in cols[1:]:
            mx = jnp.maximum(mx, c)
        m_prev = m_ref[...]
        m_new = jnp.maximum(m_prev, jnp.max(mx, axis=-1, keepdims=True))
        ps = [jnp.exp2(c - m_new) for c in cols]
        psum = ps[0]
        for p in ps[1:]:
            psum = psum + p
        for c in range(ncols):
            p_ref[:, c * LANES:(c + 1) * LANES] = ps[c].astype(BF16)
        alpha = jnp.exp2(m_prev - m_new)
        l_ref[...] = alpha * l_ref[...] + psum
        m_ref[...] = m_new
        acc_ref[...] = alpha * acc_ref[...] + jnp.dot(p_ref[:, :ncols * LANES], vblk,
                                                      preferred_element_type=F32)

    def kblock(ref, i):
        return ref[pl.ds(pl.multiple_of(i * bk, bk), bk), :]

    if n_main:
        ncols = bk // LANES
        s_bufs, p_bufs = (s0_ref, s1_ref), (p0_ref, p1_ref)
        unroll = min(ATTN_UNROLL, n_main)
        assert unroll % 2 == 0 and n_main % unroll == 0
        s0_ref[...] = scores(k_ref[0:bk, :])

        def group(j, last):
            for u in range(unroll):
                b = unroll * j + u
                if u + 1 < unroll or not last:
                    s_bufs[(u + 1) % 2][...] = scores(kblock(k_ref, b + 1))
                else:
                    s0_ref[:, :bkc] = scores(kc_ref[...])
                softmax_pv(s_bufs[u % 2], p_bufs[u % 2], kblock(v_ref, b), ncols)

        def body(j, carry):
            group(j, False)
            return carry

        lax.fori_loop(0, n_main // unroll - 1, body, 0)
        group(n_main // unroll - 1, True)
    else:
        s0_ref[...] = scores(kc_ref[...])
    softmax_pv(s0_ref, p0_ref, vc_ref[...], ncols_c)

    l_row = jnp.sum(l_ref[...], axis=-1, keepdims=True)
    o = acc_ref[...] / l_row
    for g in range(Q_PER_KV):
        o_ref[:, g * HEAD_DIM:(g + 1) * HEAD_DIM] = o[g * tq:(g + 1) * tq, :].astype(BF16)


def _attention(zq, ctx_kv, main_kv, *, tq):
    t = zq.shape[0]
    m_rows = Q_PER_KV * tq
    kv_spec = lambda length, col: pl.BlockSpec((length, HEAD_DIM), lambda h, i: (0, col + h))
    zc, kcol_c, vcol_c = ctx_kv
    bkc = zc.shape[0]
    assert bkc % LANES == 0
    args = [zq, zc, zc]
    in_specs = [pl.BlockSpec((tq, Q_PER_KV * HEAD_DIM), lambda h, i: (i, h)),
                kv_spec(bkc, kcol_c), kv_spec(bkc, vcol_c)]
    scratch = [pltpu.VMEM((m_rows, HEAD_DIM), BF16)]
    if main_kv is not None:
        zm, kcol_m, vcol_m = main_kv
        bk = ATTN_BK
        n_main = zm.shape[0] // bk
        assert n_main * bk == zm.shape[0] and bkc <= bk
        args += [zm, zm]
        in_specs += [kv_spec(zm.shape[0], kcol_m), kv_spec(zm.shape[0], vcol_m)]
        scratch += [pltpu.VMEM((m_rows, bk), F32), pltpu.VMEM((m_rows, bk), F32),
                    pltpu.VMEM((m_rows, bk), BF16), pltpu.VMEM((m_rows, bk), BF16)]
    else:
        bk, n_main = bkc, 0
        scratch += [pltpu.VMEM((m_rows, bkc), F32), pltpu.VMEM((m_rows, bkc), BF16)]
    scratch += [pltpu.VMEM((m_rows, LANES), F32), pltpu.VMEM((m_rows, LANES), F32),
                pltpu.VMEM((m_rows, HEAD_DIM), F32)]
    return pl.pallas_call(
        functools.partial(_attn_kernel, tq=tq, bk=bk, n_main=n_main, bkc=bkc),
        grid=(N_KV_HEADS, t // tq),
        in_specs=in_specs,
        out_specs=pl.BlockSpec((tq, Q_PER_KV * HEAD_DIM), lambda h, i: (i, h)),
        out_shape=jax.ShapeDtypeStruct((t, ATTN_WIDTH), BF16),
        scratch_shapes=scratch,
        compiler_params=_cparams(("arbitrary", "arbitrary")),
        name="attn",
    )(*args)


def _dft_tables(t):
    n2 = DFT_POINTS if t > 2 * DFT_POINTS else t
    n1 = t // n2
    assert n1 * n2 == t

    def cs(n):
        idx = np.arange(n)
        ang = 2.0 * np.pi * ((idx[:, None] * idx[None, :]) % n) / n
        return np.cos(ang), np.sin(ang)

    tabs = {"n1": n1, "n2": n2}
    c2, s2 = cs(n2)
    if n1 > 1:
        c1, s1 = cs(n1)
        tabs["m1"] = jnp.asarray(np.concatenate([c1, -s1], axis=0), F32)
        ang = 2.0 * np.pi * (np.arange(n1)[:, None] * np.arange(n2)[None, :]) / t
        tabs["twr"] = jnp.asarray(np.repeat(np.cos(ang), LANES, axis=1), F32)
        tabs["twi"] = jnp.asarray(np.repeat(-np.sin(ang), LANES, axis=1), F32)
        m2 = np.block([[c2, s2], [-s2, c2]])
    else:
        m2 = np.concatenate([c2, -s2], axis=0)
    tabs["m2"] = jnp.asarray(m2, F32)
    cc, sc = cs(FOURIER_GROUP_CH)
    tabs["mc"] = jnp.asarray(np.concatenate([cc, sc], axis=0), F32)
    tabs["norm"] = 1.0 / math.sqrt(t * FOURIER_GROUP_CH)
    return tabs


def _dft_a_kernel(f_ref, m1_ref, twr_ref, twi_ref, xr_ref, xi_ref, *, n1, nt2):
    w = FOURIER_WIDTH
    for j in range(nt2):
        x1 = jnp.dot(m1_ref[...], f_ref[:, j * w:(j + 1) * w], preferred_element_type=F32)
        wr = twr_ref[:, j * LANES:(j + 1) * LANES]
        wi = twi_ref[:, j * LANES:(j + 1) * LANES]
        for g in range(w // LANES):
            a = x1[:n1, g * LANES:(g + 1) * LANES]
            b = x1[n1:, g * LANES:(g + 1) * LANES]
            sl = slice(j * w + g * LANES, j * w + (g + 1) * LANES)
            xr_ref[:, sl] = (a * wr - b * wi).astype(BF16)
            xi_ref[:, sl] = (a * wi + b * wr).astype(BF16)


def _dft_b_kernel(*refs, n2, kb, complex_in, norm):
    if complex_in:
        xr_ref, xi_ref, m2_ref, mc_ref, o_ref = refs
    else:
        xr_ref, m2_ref, mc_ref, o_ref = refs
    c = FOURIER_GROUP_CH
    for j in range(kb):
        if complex_in:
            rhs = jnp.concatenate([xr_ref[j], xi_ref[j]], axis=0)
        else:
            rhs = xr_ref[j]
        p = jnp.dot(m2_ref[...], rhs, preferred_element_type=F32)
        for g in range(FOURIER_GROUPS):
            pg = jnp.concatenate([p[:n2, g * c:(g + 1) * c], p[n2:, g * c:(g + 1) * c]], axis=1)
            y = jnp.dot(pg.astype(BF16), mc_ref[...], preferred_element_type=F32) * norm
            o_ref[:, j, g * c:(g + 1) * c] = y


def _fourier(z, tabs):
    t = z.shape[0]
    n1, n2, w = tabs["n1"], tabs["n2"], FOURIER_WIDTH
    if n1 > 1:
        f = z[:, CB_FOURIER * COL_BLOCK:(CB_FOURIER + 1) * COL_BLOCK].reshape(n1, n2 * w)
        nt2 = DFT_A_T2_PER_STEP
        assert n2 % nt2 == 0
        xr, xi = pl.pallas_call(
            functools.partial(_dft_a_kernel, n1=n1, nt2=nt2),
            grid=(n2 // nt2,),
            in_specs=[
                pl.BlockSpec((n1, nt2 * w), lambda i: (0, i)),
                pl.BlockSpec((2 * n1, n1), lambda i: (0, 0)),
                pl.BlockSpec((n1, nt2 * LANES), lambda i: (0, i)),
                pl.BlockSpec((n1, nt2 * LANES), lambda i: (0, i)),
            ],
            out_specs=[pl.BlockSpec((n1, nt2 * w), lambda i: (0, i))] * 2,
            out_shape=[jax.ShapeDtypeStruct((n1, n2 * w), BF16)] * 2,
            compiler_params=_cparams(("arbitrary",)),
            name="dft_a",
        )(f, tabs["m1"].astype(BF16), tabs["twr"], tabs["twi"])
        kb = DFT_B_K1_PER_STEP
        assert n1 % kb == 0
        xs = [xr.reshape(n1, n2, w), xi.reshape(n1, n2, w)]
        x_specs = [pl.BlockSpec((kb, n2, w), lambda i: (i, 0, 0))] * 2
        complex_in = True
    else:
        kb = 1
        xs = [z.reshape(1, t, z.shape[1])]
        x_specs = [pl.BlockSpec((1, n2, w), lambda i: (0, 0, CB_FOURIER))]
        complex_in = False
    m2, mc = tabs["m2"].astype(BF16), tabs["mc"].astype(BF16)
    out = pl.pallas_call(
        functools.partial(_dft_b_kernel, n2=n2, kb=kb, complex_in=complex_in, norm=tabs["norm"]),
        grid=(n1 // kb,),
        in_specs=x_specs + [
            pl.BlockSpec(m2.shape, lambda i: (0, 0)),
            pl.BlockSpec(mc.shape, lambda i: (0, 0)),
        ],
        out_specs=pl.BlockSpec((n2, kb, w), lambda i: (0, i, 0)),
        out_shape=jax.ShapeDtypeStruct((n2, n1, w), F32),
        compiler_params=_cparams(("arbitrary",)),
        name="dft_b",
    )(*xs, m2, mc)
    return out.reshape(t, w)


def _merge_kernel(a_ref, u_ref, vn_ref, fo_ref, g_ref, x_ref, wa_ref, ws_ref, wf_ref, wo_ref,
                  wsp_ref, bsp_ref, gate_ref, lg_ref, lb_ref, o_ref, gout_ref, *, tm):
    group_ch = SGU_WIDTH // SGU_GROUPS
    lane = lax.broadcasted_iota(jnp.int32, (SGU_CHUNK, LANES), 1)
    for ci in range(tm // SGU_CHUNK):
        rows = slice(ci * SGU_CHUNK, (ci + 1) * SGU_CHUNK)
        for lb in range(SGU_WIDTH // LANES):
            cols = slice(lb * LANES, (lb + 1) * LANES)
            y = jnp.dot(wsp_ref[lb], vn_ref[rows, cols], preferred_element_type=F32)
            mixed = jnp.where(lane < group_ch, y[:SGU_CHUNK], y[SGU_CHUNK:]) + bsp_ref[:, cols]
            gout_ref[rows, cols] = (u_ref[rows, cols].astype(F32) * mixed).astype(BF16)

    d = D_MODEL
    merged = g_ref[:, 0:d].astype(F32) * jnp.dot(a_ref[...], wa_ref[...], preferred_element_type=F32)
    merged += g_ref[:, d:2 * d].astype(F32) * jnp.dot(gout_ref[...], ws_ref[...],
                                                      preferred_element_type=F32)
    merged += g_ref[:, 2 * d:3 * d].astype(F32) * jnp.dot(fo_ref[...].astype(BF16), wf_ref[...],
                                                          preferred_element_type=F32)
    mix = jnp.dot(merged.astype(BF16), wo_ref[...], preferred_element_type=F32)
    y = DEEPNORM_ALPHA * x_ref[...] + gate_ref[...] * mix
    o_ref[...] = _ln(y) * lg_ref[...] + lb_ref[...]


def _merge(a, z, f_out, x, w_br_attn, w_br_sgu, w_br_fourier, w_out, wsp2, bsp, gate1, ln_g, ln_b, *, tm):
    t, d = x.shape
    full = lambda arr: pl.BlockSpec(arr.shape, lambda i: (0,) * arr.ndim)
    vec = pl.BlockSpec((1, d), lambda i: (0, 0))
    return pl.pallas_call(
        functools.partial(_merge_kernel, tm=tm),
        grid=(t // tm,),
        in_specs=[
            pl.BlockSpec((tm, ATTN_WIDTH), lambda i: (i, 0)),
            pl.BlockSpec((tm, COL_BLOCK), lambda i: (i, CB_SGU_U)),
            pl.BlockSpec((tm, COL_BLOCK), lambda i: (i, CB_SGU_V)),
            pl.BlockSpec((tm, FOURIER_WIDTH), lambda i: (i, 0)),
            pl.BlockSpec((tm, N_BRANCHES * d), lambda i: (i, CB_GATES * COL_BLOCK // (N_BRANCHES * d))),
            pl.BlockSpec((tm, d), lambda i: (i, 0)),
            full(w_br_attn), full(w_br_sgu), full(w_br_fourier), full(w_out),
            full(wsp2), full(bsp), vec, vec, vec,
        ],
        out_specs=pl.BlockSpec((tm, d), lambda i: (i, 0)),
        out_shape=jax.ShapeDtypeStruct((t, d), F32),
        scratch_shapes=[pltpu.VMEM((tm, SGU_WIDTH), BF16)],
        compiler_params=_cparams(("arbitrary",)),
        name="merge",
    )(a, z, z, f_out, z, x, w_br_attn, w_br_sgu, w_br_fourier, w_out, wsp2, bsp, gate1, ln_g, ln_b)


def _ffn_kernel(x_ref, sh_ref, sc_ref, gate_ref, wup_ref, wd_ref, lg_ref, lb_ref, o_ref, *, hidden):
    x = x_ref[...]
    h = (_ln(x) * (1.0 + sc_ref[...]) + sh_ref[...]).astype(BF16)
    th = hidden // FFN_CHUNKS
    acc = None
    for c in range(FFN_CHUNKS):
        g = jnp.dot(h, wup_ref[:, c * th:(c + 1) * th], preferred_element_type=F32)
        u = jnp.dot(h, wup_ref[:, hidden + c * th:hidden + (c + 1) * th],
                    preferred_element_type=F32)
        act = (g * _sigmoid(g) * u).astype(BF16)
        part = jnp.dot(act, wd_ref[c * th:(c + 1) * th, :], preferred_element_type=F32)
        acc = part if acc is None else acc + part
    y = DEEPNORM_ALPHA * x + gate_ref[...] * acc
    o_ref[...] = _ln(y) * lg_ref[...] + lb_ref[...]


def _ffn(x, shift, scale, gate, w_up, w_down, ln_g, ln_b, *, tm):
    t, d = x.shape
    hidden = w_down.shape[0]
    th = hidden // FFN_CHUNKS
    assert th * FFN_CHUNKS == hidden and th % LANES == 0
    vec = pl.BlockSpec((1, d), lambda i: (0, 0))
    resident = lambda arr: pl.BlockSpec(arr.shape, lambda i: (0, 0), pipeline_mode=pl.Buffered(1))
    return pl.pallas_call(
        functools.partial(_ffn_kernel, hidden=hidden),
        grid=(t // tm,),
        in_specs=[
            pl.BlockSpec((tm, d), lambda i: (i, 0)),
            vec, vec, vec,
            resident(w_up), resident(w_down),
            vec, vec,
        ],
        out_specs=pl.BlockSpec((tm, d), lambda i: (i, 0)),
        out_shape=jax.ShapeDtypeStruct((t, d), F32),
        compiler_params=_cparams(("arbitrary",)),
        name="ffn",
    )(x, shift, scale, gate, w_up, w_down, ln_g, ln_b)


def _rope_tables(n_tokens):
    rows = n_tokens // GRID_W
    pos_r = np.repeat(np.arange(rows, dtype=np.float64), GRID_W)
    pos_c = np.tile(np.arange(GRID_W, dtype=np.float64), rows)
    inv = ROPE_THETA ** (-np.arange(0, ROPE_AXIS_DIM, 2, dtype=np.float64) / ROPE_AXIS_DIM)
    ang = np.concatenate([pos_r[:, None] * inv, pos_c[:, None] * inv], axis=-1)
    cos, sin = np.cos(ang), np.sin(ang)
    cos2 = np.repeat(cos, 2, axis=-1)
    sin2 = np.stack([-sin, sin], axis=-1).reshape(n_tokens, HEAD_DIM)
    return jnp.asarray(cos2, F32), jnp.asarray(sin2, F32)


def kernel(x, c, ctx, c_ctx, w_ada, b_ada, w_in, q_gain, k_gain, sgu_ln_g, sgu_ln_b, w_spatial,
           b_spatial, w_br_attn, w_br_sgu, w_br_fourier, w_out, ln1_g, ln1_b, w_up, w_down,
           ln2_g, ln2_b):
    batch, seq, d = x.shape
    n_ctx = ctx.shape[1]
    depth = w_in.shape[0]
    assert batch == 1 and d == D_MODEL and depth == DEPTH
    xl = x[0]
    xc = ctx[0]

    rope = _rope_tables(seq)
    tabs_x = _dft_tables(seq)
    tabs_c = _dft_tables(n_ctx)

    c2 = jnp.stack([c[0], c_ctx], axis=1)
    mod = _ada(c2, w_ada, b_ada)

    tm_x = _row_tile(seq, ROW_TILE)
    tm_c = _row_tile(n_ctx, CTX_ROW_TILE)
    tq_x = _row_tile(seq, ATTN_TQ)
    tq_c = _row_tile(n_ctx, ATTN_TQ)

    row = lambda v: v.reshape(1, -1)
    kcol = CB_KV * COL_BLOCK // HEAD_DIM
    vcol = kcol + N_KV_HEADS

    for l in range(depth):
        last = l == depth - 1
        mx = [mod[l, 0:1, i * d:(i + 1) * d] for i in range(6)]
        mc = [mod[l, 1:2, i * d:(i + 1) * d] for i in range(6)]
        w_in_l = w_in[l].astype(BF16)
        qg, kg = row(q_gain[l]), row(k_gain[l])
        sg, sb = row(sgu_ln_g[l]), row(sgu_ln_b[l])
        wsp2 = w_spatial[l].astype(BF16).reshape(SGU_GROUPS // 2, 2 * SGU_CHUNK, SGU_CHUNK)
        bsp = jnp.repeat(b_spatial[l].T, SGU_WIDTH // SGU_GROUPS, axis=1)
        mixer_w = (w_br_attn[l].astype(BF16), w_br_sgu[l].astype(BF16), w_br_fourier[l].astype(BF16),
                   w_out[l].astype(BF16), wsp2, bsp)
        w_up_l, w_down_l = w_up[l].astype(BF16), w_down[l].astype(BF16)
        l1g, l1b, l2g, l2b = row(ln1_g[l]), row(ln1_b[l]), row(ln2_g[l]), row(ln2_b[l])

        if last:
            zc = _in_proj(xc, mc[0], mc[1], w_in_l, qg, kg, sg, sb, None, cb0=CB_KV, ncb=1, tm=tm_c)
            ctx_kv = (zc, 0, N_KV_HEADS)
        else:
            zc = _in_proj(xc, mc[0], mc[1], w_in_l, qg, kg, sg, sb, None, cb0=0, ncb=N_COL_BLOCKS, tm=tm_c)
            ctx_kv = (zc, kcol, vcol)
            ac = _attention(zc, ctx_kv, None, tq=tq_c)
            fc = _fourier(zc, tabs_c)
            xc1 = _merge(ac, zc, fc, xc, *mixer_w, mc[2], l1g, l1b, tm=tm_c)
            xc = _ffn(xc1, mc[3], mc[4], mc[5], w_up_l, w_down_l, l2g, l2b, tm=tm_c)

        zx = _in_proj(xl, mx[0], mx[1], w_in_l, qg, kg, sg, sb, rope, cb0=0, ncb=N_COL_BLOCKS, tm=tm_x)
        ax = _attention(zx, ctx_kv, (zx, kcol, vcol), tq=tq_x)
        fx = _fourier(zx, tabs_x)
        xl1 = _merge(ax, zx, fx, xl, *mixer_w, mx[2], l1g, l1b, tm=tm_x)
        xl = _ffn(xl1, mx[3], mx[4], mx[5], w_up_l, w_down_l, l2g, l2b, tm=tm_x)

    return xl[None]
```

```python
import functools
import math

import jax
import jax.numpy as jnp
import numpy as np
from jax import lax
from jax.experimental import pallas as pl
from jax.experimental.pallas import tpu as pltpu

F32 = jnp.float32
BF16 = jnp.bfloat16

D_MODEL = 1024
DEPTH = 2
GRID_W = 64
N_Q_HEADS = 8
N_KV_HEADS = 2
HEAD_DIM = 128
Q_PER_KV = N_Q_HEADS // N_KV_HEADS
ATTN_WIDTH = N_Q_HEADS * HEAD_DIM
KV_WIDTH = N_KV_HEADS * HEAD_DIM
ROPE_AXIS_DIM = HEAD_DIM // 2
ROPE_THETA = 10000.0
SGU_GROUPS = 8
SGU_WIDTH = 512
SGU_CHUNK = 128
FOURIER_GROUPS = 4
FOURIER_WIDTH = 512
FOURIER_GROUP_CH = FOURIER_WIDTH // FOURIER_GROUPS
N_BRANCHES = 3
FFN_HIDDEN = -(-8 * D_MODEL // (3 * 256)) * 256
IN_WIDTH = ATTN_WIDTH + 2 * KV_WIDTH + 2 * SGU_WIDTH + FOURIER_WIDTH + N_BRANCHES * D_MODEL
DEEPNORM_ALPHA = (2 * DEPTH) ** 0.25
LN_EPS = 1e-6
RMS_EPS = 1e-6
ATTN_SCALE = HEAD_DIM ** -0.5
Q_PRESCALE = ATTN_SCALE * math.log2(math.e)

COL_BLOCK = 512
N_COL_BLOCKS = IN_WIDTH // COL_BLOCK
CB_KV = 2
CB_SGU_U = 3
CB_SGU_V = 4
CB_FOURIER = 5
CB_GATES = 6

LANES = 128
V7X_VMEM_BYTES = 64 * 1024 * 1024
VMEM_LIMIT_BYTES = V7X_VMEM_BYTES - 12 * 1024 * 1024

ADA_TN = 1024
ROW_TILE = 512
CTX_ROW_TILE = 256
ATTN_TQ = 256
ATTN_BK = 512
ATTN_UNROLL = 8
FFN_CHUNKS = 2
DFT_POINTS = 128
DFT_A_T2_PER_STEP = 8
DFT_B_K1_PER_STEP = 8


def _cparams(sem):
    return pltpu.CompilerParams(dimension_semantics=sem, vmem_limit_bytes=VMEM_LIMIT_BYTES)


def _row_tile(t, want):
    tm = min(t, want)
    assert t % tm == 0
    return tm


def _sigmoid(x):
    return 1.0 / (1.0 + jnp.exp(-x))


def _gelu_tanh(x):
    c = math.sqrt(2.0 / math.pi)
    return 0.5 * x * (1.0 + jnp.tanh(c * (x + 0.044715 * (x * x * x))))


def _ln(x):
    mu = jnp.mean(x, axis=-1, keepdims=True)
    xc = x - mu
    var = jnp.mean(xc * xc, axis=-1, keepdims=True)
    return xc * lax.rsqrt(var + LN_EPS)


def _ada_kernel(c_ref, w_ref, b_ref, o_ref):
    c = c_ref[...]
    s = c * _sigmoid(c)
    w = w_ref[0]
    b = b_ref[0]
    r0 = jnp.sum(w * s[:, 0:1], axis=0, keepdims=True) + b
    r1 = jnp.sum(w * s[:, 1:2], axis=0, keepdims=True) + b
    o_ref[0] = jnp.concatenate([r0, r1], axis=0)


def _ada(c2, w_ada, b_ada):
    depth, d, n = w_ada.shape
    tn = ADA_TN
    return pl.pallas_call(
        _ada_kernel,
        grid=(depth, n // tn),
        in_specs=[
            pl.BlockSpec((d, 2), lambda l, j: (0, 0)),
            pl.BlockSpec((1, d, tn), lambda l, j: (l, 0, j)),
            pl.BlockSpec((1, 1, tn), lambda l, j: (l, 0, j)),
        ],
        out_specs=pl.BlockSpec((1, 2, tn), lambda l, j: (l, 0, j)),
        out_shape=jax.ShapeDtypeStruct((depth, 2, n), F32),
        compiler_params=_cparams(("arbitrary", "arbitrary")),
        name="ada",
    )(c2, w_ada, b_ada.reshape(depth, 1, n))


def _rms_rope(zh, gain, cos, sin_signed, use_rope):
    y = zh * lax.rsqrt(jnp.mean(zh * zh, axis=-1, keepdims=True) + RMS_EPS) * gain
    if not use_rope:
        return y
    lane = lax.broadcasted_iota(jnp.int32, y.shape, 1)
    nxt = pltpu.roll(y, HEAD_DIM - 1, axis=1)
    prv = pltpu.roll(y, 1, axis=1)
    partner = jnp.where((lane & 1) == 0, nxt, prv)
    return y * cos + partner * sin_signed


def _in_kernel(x_ref, sh_ref, sc_ref, w_ref, qg_ref, kg_ref, lg_ref, lb_ref, cos_ref, sin_ref,
               o_ref, *, cb0, ncb, use_rope):
    h = (_ln(x_ref[...]) * (1.0 + sc_ref[...]) + sh_ref[...]).astype(BF16)
    cos = cos_ref[...] if use_rope else None
    sin = sin_ref[...] if use_rope else None

    def qk_heads(z, base, count, gain):
        for hh in range(count):
            sl = slice(hh * HEAD_DIM, (hh + 1) * HEAD_DIM)
            o_ref[:, base + hh * HEAD_DIM:base + (hh + 1) * HEAD_DIM] = _rms_rope(
                z[:, sl], gain, cos, sin, use_rope).astype(BF16)

    for jj in range(ncb):
        cb = cb0 + jj
        base = jj * COL_BLOCK
        cols = slice(base, base + COL_BLOCK)
        z = jnp.dot(h, w_ref[:, cols], preferred_element_type=F32)
        if cb < CB_KV:
            qk_heads(z, base, COL_BLOCK // HEAD_DIM, qg_ref[...] * Q_PRESCALE)
        elif cb == CB_KV:
            qk_heads(z, base, N_KV_HEADS, kg_ref[...])
            o_ref[:, base + KV_WIDTH:base + COL_BLOCK] = z[:, KV_WIDTH:].astype(BF16)
        elif cb == CB_SGU_U:
            o_ref[:, cols] = _gelu_tanh(z).astype(BF16)
        elif cb == CB_SGU_V:
            o_ref[:, cols] = (_ln(_gelu_tanh(z)) * lg_ref[...] + lb_ref[...]).astype(BF16)
        elif cb == CB_FOURIER:
            o_ref[:, cols] = z.astype(BF16)
        else:
            o_ref[:, cols] = _sigmoid(z).astype(BF16)


def _in_proj(x, shift, scale, w_in, q_gain, k_gain, sgu_g, sgu_b, rope, *, cb0, ncb, tm):
    t, d = x.shape
    assert cb0 % ncb == 0
    width = ncb * COL_BLOCK
    use_rope = rope is not None
    if use_rope:
        cos, sin = rope
    else:
        cos = sin = jnp.zeros((8, HEAD_DIM), F32)
    rope_spec = (pl.BlockSpec((tm, HEAD_DIM), lambda i: (i, 0)) if use_rope
                 else pl.BlockSpec((8, HEAD_DIM), lambda i: (0, 0)))
    vec = lambda n: pl.BlockSpec((1, n), lambda i: (0, 0))
    return pl.pallas_call(
        functools.partial(_in_kernel, cb0=cb0, ncb=ncb, use_rope=use_rope),
        grid=(t // tm,),
        in_specs=[
            pl.BlockSpec((tm, d), lambda i: (i, 0)),
            vec(d), vec(d),
            pl.BlockSpec((d, width), lambda i: (0, cb0 // ncb), pipeline_mode=pl.Buffered(1)),
            vec(HEAD_DIM), vec(HEAD_DIM), vec(SGU_WIDTH), vec(SGU_WIDTH),
            rope_spec, rope_spec,
        ],
        out_specs=pl.BlockSpec((tm, width), lambda i: (i, 0)),
        out_shape=jax.ShapeDtypeStruct((t, width), BF16),
        compiler_params=_cparams(("arbitrary",)),
        name="in_proj",
    )(x, shift, scale, w_in, q_gain, k_gain, sgu_g, sgu_b, cos, sin)


def _attn_kernel(*refs, tq, bk, n_main, bkc):
    if n_main:
        q_ref, kc_ref, vc_ref, k_ref, v_ref, o_ref = refs[:6]
        qs_ref, s0_ref, s1_ref, p0_ref, p1_ref, m_ref, l_ref, acc_ref = refs[6:]
    else:
        q_ref, kc_ref, vc_ref, o_ref = refs[:4]
        qs_ref, s0_ref, p0_ref, m_ref, l_ref, acc_ref = refs[4:]
    ncols_c = bkc // LANES

    for g in range(Q_PER_KV):
        qs_ref[g * tq:(g + 1) * tq, :] = q_ref[:, g * HEAD_DIM:(g + 1) * HEAD_DIM]
    m_ref[...] = jnp.full(m_ref.shape, -jnp.inf, F32)
    l_ref[...] = jnp.zeros(l_ref.shape, F32)
    acc_ref[...] = jnp.zeros(acc_ref.shape, F32)

    def scores(kblk_t):
        return jnp.dot(qs_ref[...], kblk_t, preferred_element_type=F32)

    def softmax_pv(s_ref, p_ref, vblk, ncols):
        cols = [s_ref[:, c * LANES:(c + 1) * LANES] for c in range(ncols)]
        mx = cols[0]
        for c in cols[1:]:
            mx = jnp.maximum(mx, c)
        m_prev = m_ref[...]
        m_new = jnp.maximum(m_prev, jnp.max(mx, axis=-1, keepdims=True))
        ps = [jnp.exp2(c - m_new) for c in cols]
        psum = ps[0]
        for p in ps[1:]:
            psum = psum + p
        for c in range(ncols):
            p_ref[:, c * LANES:(c + 1) * LANES] = ps[c].astype(BF16)
        alpha = jnp.exp2(m_prev - m_new)
        l_ref[...] = alpha * l_ref[...] + psum
        m_ref[...] = m_new
        acc_ref[...] = alpha * acc_ref[...] + jnp.dot(p_ref[:, :ncols * LANES], vblk,
                                                      preferred_element_type=F32)

    def kblock(ref, i):
        return ref[pl.ds(pl.multiple_of(i * bk, bk), bk), :]

    if n_main:
        ncols = bk // LANES
        s_bufs, p_bufs = (s0_ref, s1_ref), (p0_ref, p1_ref)
        unroll = min(ATTN_UNROLL, n_main)
        assert unroll % 2 == 0 and n_main % unroll == 0
        s0_ref[...] = scores(k_ref[0])

        def group(j, last):
            for u in range(unroll):
                b = unroll * j + u
                if u + 1 < unroll or not last:
                    s_bufs[(u + 1) % 2][...] = scores(k_ref[b + 1])
                else:
                    s0_ref[:, :bkc] = scores(kc_ref[0])
                softmax_pv(s_bufs[u % 2], p_bufs[u % 2], kblock(v_ref, b), ncols)

        def body(j, carry):
            group(j, False)
            return carry

        lax.fori_loop(0, n_main // unroll - 1, body, 0)
        group(n_main // unroll - 1, True)
    else:
        s0_ref[...] = scores(kc_ref[0])
    softmax_pv(s0_ref, p0_ref, vc_ref[...], ncols_c)

    l_row = jnp.sum(l_ref[...], axis=-1, keepdims=True)
    o = acc_ref[...] / l_row
    for g in range(Q_PER_KV):
        o_ref[:, g * HEAD_DIM:(g + 1) * HEAD_DIM] = o[g * tq:(g + 1) * tq, :].astype(BF16)


def _k_blocks_t(z, kcol, bk):
    t = z.shape[0]
    k = z[:, kcol * HEAD_DIM:(kcol + N_KV_HEADS) * HEAD_DIM].reshape(t // bk, bk, N_KV_HEADS, HEAD_DIM)
    return k.transpose(2, 0, 3, 1)


def _attention(zq, ctx_kv, main_kv, *, tq):
    t = zq.shape[0]
    m_rows = Q_PER_KV * tq
    kv_spec = lambda length, col: pl.BlockSpec((length, HEAD_DIM), lambda h, i: (0, col + h))
    kt_spec = lambda nblk, bkk: pl.BlockSpec((None, nblk, HEAD_DIM, bkk), lambda h, i: (h, 0, 0, 0))
    zc, kcol_c, vcol_c = ctx_kv
    bkc = zc.shape[0]
    assert bkc % LANES == 0
    args = [zq, _k_blocks_t(zc, kcol_c, bkc), zc]
    in_specs = [pl.BlockSpec((tq, Q_PER_KV * HEAD_DIM), lambda h, i: (i, h)),
                kt_spec(1, bkc), kv_spec(bkc, vcol_c)]
    scratch = [pltpu.VMEM((m_rows, HEAD_DIM), BF16)]
    if main_kv is not None:
        zm, kcol_m, vcol_m = main_kv
        bk = ATTN_BK
        n_main = zm.shape[0] // bk
        assert n_main * bk == zm.shape[0] and bkc <= bk
        args += [_k_blocks_t(zm, kcol_m, bk), zm]
        in_specs += [kt_spec(n_main, bk), kv_spec(zm.shape[0], vcol_m)]
        scratch += [pltpu.VMEM((m_rows, bk), F32), pltpu.VMEM((m_rows, bk), F32),
                    pltpu.VMEM((m_rows, bk), BF16), pltpu.VMEM((m_rows, bk), BF16)]
    else:
        bk, n_main = bkc, 0
        scratch += [pltpu.VMEM((m_rows, bkc), F32), pltpu.VMEM((m_rows, bkc), BF16)]
    scratch += [pltpu.VMEM((m_rows, LANES), F32), pltpu.VMEM((m_rows, LANES), F32),
                pltpu.VMEM((m_rows, HEAD_DIM), F32)]
    return pl.pallas_call(
        functools.partial(_attn_kernel, tq=tq, bk=bk, n_main=n_main, bkc=bkc),
        grid=(N_KV_HEADS, t // tq),
        in_specs=in_specs,
        out_specs=pl.BlockSpec((tq, Q_PER_KV * HEAD_DIM), lambda h, i: (i, h)),
        out_shape=jax.ShapeDtypeStruct((t, ATTN_WIDTH), BF16),
        scratch_shapes=scratch,
        compiler_params=_cparams(("arbitrary", "arbitrary")),
        name="attn",
    )(*args)


def _dft_tables(t):
    n2 = DFT_POINTS if t > 2 * DFT_POINTS else t
    n1 = t // n2
    assert n1 * n2 == t

    def cs(n):
        idx = np.arange(n)
        ang = 2.0 * np.pi * ((idx[:, None] * idx[None, :]) % n) / n
        return np.cos(ang), np.sin(ang)

    tabs = {"n1": n1, "n2": n2}
    c2, s2 = cs(n2)
    if n1 > 1:
        c1, s1 = cs(n1)
        tabs["m1"] = jnp.asarray(np.concatenate([c1, -s1], axis=0), F32)
        ang = 2.0 * np.pi * (np.arange(n1)[:, None] * np.arange(n2)[None, :]) / t
        tabs["twr"] = jnp.asarray(np.repeat(np.cos(ang), LANES, axis=1), F32)
        tabs["twi"] = jnp.asarray(np.repeat(-np.sin(ang), LANES, axis=1), F32)
        m2 = np.block([[c2, s2], [-s2, c2]])
    else:
        m2 = np.concatenate([c2, -s2], axis=0)
    tabs["m2"] = jnp.asarray(m2, F32)
    cc, sc = cs(FOURIER_GROUP_CH)
    tabs["mc"] = jnp.asarray(np.concatenate([cc, sc], axis=0), F32)
    tabs["norm"] = 1.0 / math.sqrt(t * FOURIER_GROUP_CH)
    return tabs


def _dft_a_kernel(f_ref, m1_ref, twr_ref, twi_ref, xr_ref, xi_ref, *, n1, nt2):
    w = FOURIER_WIDTH
    for j in range(nt2):
        x1 = jnp.dot(m1_ref[...], f_ref[:, j * w:(j + 1) * w], preferred_element_type=F32)
        wr = twr_ref[:, j * LANES:(j + 1) * LANES]
        wi = twi_ref[:, j * LANES:(j + 1) * LANES]
        for g in range(w // LANES):
            a = x1[:n1, g * LANES:(g + 1) * LANES]
            b = x1[n1:, g * LANES:(g + 1) * LANES]
            sl = slice(j * w + g * LANES, j * w + (g + 1) * LANES)
            xr_ref[:, sl] = (a * wr - b * wi).astype(BF16)
            xi_ref[:, sl] = (a * wi + b * wr).astype(BF16)


def _dft_b_kernel(*refs, n2, kb, complex_in, norm):
    if complex_in:
        xr_ref, xi_ref, m2_ref, mc_ref, o_ref = refs
    else:
        xr_ref, m2_ref, mc_ref, o_ref = refs
    c = FOURIER_GROUP_CH
    for j in range(kb):
        if complex_in:
            rhs = jnp.concatenate([xr_ref[j], xi_ref[j]], axis=0)
        else:
            rhs = xr_ref[j]
        p = jnp.dot(m2_ref[...], rhs, preferred_element_type=F32)
        for g in range(FOURIER_GROUPS):
            pg = jnp.concatenate([p[:n2, g * c:(g + 1) * c], p[n2:, g * c:(g + 1) * c]], axis=1)
            y = jnp.dot(pg.astype(BF16), mc_ref[...], preferred_element_type=F32) * norm
            o_ref[:, j, g * c:(g + 1) * c] = y


def _fourier(z, tabs):
    t = z.shape[0]
    n1, n2, w = tabs["n1"], tabs["n2"], FOURIER_WIDTH
    if n1 > 1:
        f = z[:, CB_FOURIER * COL_BLOCK:(CB_FOURIER + 1) * COL_BLOCK].reshape(n1, n2 * w)
        nt2 = DFT_A_T2_PER_STEP
        assert n2 % nt2 == 0
        xr, xi = pl.pallas_call(
            functools.partial(_dft_a_kernel, n1=n1, nt2=nt2),
            grid=(n2 // nt2,),
            in_specs=[
                pl.BlockSpec((n1, nt2 * w), lambda i: (0, i)),
                pl.BlockSpec((2 * n1, n1), lambda i: (0, 0)),
                pl.BlockSpec((n1, nt2 * LANES), lambda i: (0, i)),
                pl.BlockSpec((n1, nt2 * LANES), lambda i: (0, i)),
            ],
            out_specs=[pl.BlockSpec((n1, nt2 * w), lambda i: (0, i))] * 2,
            out_shape=[jax.ShapeDtypeStruct((n1, n2 * w), BF16)] * 2,
            compiler_params=_cparams(("arbitrary",)),
            name="dft_a",
        )(f, tabs["m1"].astype(BF16), tabs["twr"], tabs["twi"])
        kb = DFT_B_K1_PER_STEP
        assert n1 % kb == 0
        xs = [xr.reshape(n1, n2, w), xi.reshape(n1, n2, w)]
        x_specs = [pl.BlockSpec((kb, n2, w), lambda i: (i, 0, 0))] * 2
        complex_in = True
    else:
        kb = 1
        xs = [z.reshape(1, t, z.shape[1])]
        x_specs = [pl.BlockSpec((1, n2, w), lambda i: (0, 0, CB_FOURIER))]
        complex_in = False
    m2, mc = tabs["m2"].astype(BF16), tabs["mc"].astype(BF16)
    out = pl.pallas_call(
        functools.partial(_dft_b_kernel, n2=n2, kb=kb, complex_in=complex_in, norm=tabs["norm"]),
        grid=(n1 // kb,),
        in_specs=x_specs + [
            pl.BlockSpec(m2.shape, lambda i: (0, 0)),
            pl.BlockSpec(mc.shape, lambda i: (0, 0)),
        ],
        out_specs=pl.BlockSpec((n2, kb, w), lambda i: (0, i, 0)),
        out_shape=jax.ShapeDtypeStruct((n2, n1, w), F32),
        compiler_params=_cparams(("arbitrary",)),
        name="dft_b",
    )(*xs, m2, mc)
    return out.reshape(t, w)


def _merge_kernel(a_ref, u_ref, vn_ref, fo_ref, g_ref, x_ref, wa_ref, ws_ref, wf_ref, wo_ref,
                  wsp_ref, bsp_ref, gate_ref, lg_ref, lb_ref, o_ref, gout_ref, *, tm):
    group_ch = SGU_WIDTH // SGU_GROUPS
    lane = lax.broadcasted_iota(jnp.int32, (SGU_CHUNK, LANES), 1)
    for ci in range(tm // SGU_CHUNK):
        rows = slice(ci * SGU_CHUNK, (ci + 1) * SGU_CHUNK)
        for lb in range(SGU_WIDTH // LANES):
            cols = slice(lb * LANES, (lb + 1) * LANES)
            y = jnp.dot(wsp_ref[lb], vn_ref[rows, cols], preferred_element_type=F32)
            mixed = jnp.where(lane < group_ch, y[:SGU_CHUNK], y[SGU_CHUNK:]) + bsp_ref[:, cols]
            gout_ref[rows, cols] = (u_ref[rows, cols].astype(F32) * mixed).astype(BF16)

    d = D_MODEL
    merged = g_ref[:, 0:d].astype(F32) * jnp.dot(a_ref[...], wa_ref[...], preferred_element_type=F32)
    merged += g_ref[:, d:2 * d].astype(F32) * jnp.dot(gout_ref[...], ws_ref[...],
                                                      preferred_element_type=F32)
    merged += g_ref[:, 2 * d:3 * d].astype(F32) * jnp.dot(fo_ref[...].astype(BF16), wf_ref[...],
                                                          preferred_element_type=F32)
    mix = jnp.dot(merged.astype(BF16), wo_ref[...], preferred_element_type=F32)
    y = DEEPNORM_ALPHA * x_ref[...] + gate_ref[...] * mix
    o_ref[...] = _ln(y) * lg_ref[...] + lb_ref[...]


def _merge(a, z, f_out, x, w_br_attn, w_br_sgu, w_br_fourier, w_out, wsp2, bsp, gate1, ln_g, ln_b, *, tm):
    t, d = x.shape
    full = lambda arr: pl.BlockSpec(arr.shape, lambda i: (0,) * arr.ndim)
    vec = pl.BlockSpec((1, d), lambda i: (0, 0))
    return pl.pallas_call(
        functools.partial(_merge_kernel, tm=tm),
        grid=(t // tm,),
        in_specs=[
            pl.BlockSpec((tm, ATTN_WIDTH), lambda i: (i, 0)),
            pl.BlockSpec((tm, COL_BLOCK), lambda i: (i, CB_SGU_U)),
            pl.BlockSpec((tm, COL_BLOCK), lambda i: (i, CB_SGU_V)),
            pl.BlockSpec((tm, FOURIER_WIDTH), lambda i: (i, 0)),
            pl.BlockSpec((tm, N_BRANCHES * d), lambda i: (i, CB_GATES * COL_BLOCK // (N_BRANCHES * d))),
            pl.BlockSpec((tm, d), lambda i: (i, 0)),
            full(w_br_attn), full(w_br_sgu), full(w_br_fourier), full(w_out),
            full(wsp2), full(bsp), vec, vec, vec,
        ],
        out_specs=pl.BlockSpec((tm, d), lambda i: (i, 0)),
        out_shape=jax.ShapeDtypeStruct((t, d), F32),
        scratch_shapes=[pltpu.VMEM((tm, SGU_WIDTH), BF16)],
        compiler_params=_cparams(("arbitrary",)),
        name="merge",
    )(a, z, z, f_out, z, x, w_br_attn, w_br_sgu, w_br_fourier, w_out, wsp2, bsp, gate1, ln_g, ln_b)


def _ffn_kernel(x_ref, sh_ref, sc_ref, gate_ref, wup_ref, wd_ref, lg_ref, lb_ref, o_ref, *, hidden):
    x = x_ref[...]
    h = (_ln(x) * (1.0 + sc_ref[...]) + sh_ref[...]).astype(BF16)
    th = hidden // FFN_CHUNKS
    acc = None
    for c in range(FFN_CHUNKS):
        g = jnp.dot(h, wup_ref[:, c * th:(c + 1) * th], preferred_element_type=F32)
        u = jnp.dot(h, wup_ref[:, hidden + c * th:hidden + (c + 1) * th],
                    preferred_element_type=F32)
        act = (g * _sigmoid(g) * u).astype(BF16)
        part = jnp.dot(act, wd_ref[c * th:(c + 1) * th, :], preferred_element_type=F32)
        acc = part if acc is None else acc + part
    y = DEEPNORM_ALPHA * x + gate_ref[...] * acc
    o_ref[...] = _ln(y) * lg_ref[...] + lb_ref[...]


def _ffn(x, shift, scale, gate, w_up, w_down, ln_g, ln_b, *, tm):
    t, d = x.shape
    hidden = w_down.shape[0]
    th = hidden // FFN_CHUNKS
    assert th * FFN_CHUNKS == hidden and th % LANES == 0
    vec = pl.BlockSpec((1, d), lambda i: (0, 0))
    resident = lambda arr: pl.BlockSpec(arr.shape, lambda i: (0, 0), pipeline_mode=pl.Buffered(1))
    return pl.pallas_call(
        functools.partial(_ffn_kernel, hidden=hidden),
        grid=(t // tm,),
        in_specs=[
            pl.BlockSpec((tm, d), lambda i: (i, 0)),
            vec, vec, vec,
            resident(w_up), resident(w_down),
            vec, vec,
        ],
        out_specs=pl.BlockSpec((tm, d), lambda i: (i, 0)),
        out_shape=jax.ShapeDtypeStruct((t, d), F32),
        compiler_params=_cparams(("arbitrary",)),
        name="ffn",
    )(x, shift, scale, gate, w_up, w_down, ln_g, ln_b)


def _rope_tables(n_tokens):
    rows = n_tokens // GRID_W
    pos_r = np.repeat(np.arange(rows, dtype=np.float64), GRID_W)
    pos_c = np.tile(np.arange(GRID_W, dtype=np.float64), rows)
    inv = ROPE_THETA ** (-np.arange(0, ROPE_AXIS_DIM, 2, dtype=np.float64) / ROPE_AXIS_DIM)
    ang = np.concatenate([pos_r[:, None] * inv, pos_c[:, None] * inv], axis=-1)
    cos, sin = np.cos(ang), np.sin(ang)
    cos2 = np.repeat(cos, 2, axis=-1)
    sin2 = np.stack([-sin, sin], axis=-1).reshape(n_tokens, HEAD_DIM)
    return jnp.asarray(cos2, F32), jnp.asarray(sin2, F32)


def kernel(x, c, ctx, c_ctx, w_ada, b_ada, w_in, q_gain, k_gain, sgu_ln_g, sgu_ln_b, w_spatial,
           b_spatial, w_br_attn, w_br_sgu, w_br_fourier, w_out, ln1_g, ln1_b, w_up, w_down,
           ln2_g, ln2_b):
    batch, seq, d = x.shape
    n_ctx = ctx.shape[1]
    depth = w_in.shape[0]
    assert batch == 1 and d == D_MODEL and depth == DEPTH
    xl = x[0]
    xc = ctx[0]

    rope = _rope_tables(seq)
    tabs_x = _dft_tables(seq)
    tabs_c = _dft_tables(n_ctx)

    c2 = jnp.stack([c[0], c_ctx], axis=1)
    mod = _ada(c2, w_ada, b_ada)

    tm_x = _row_tile(seq, ROW_TILE)
    tm_c = _row_tile(n_ctx, CTX_ROW_TILE)
    tq_x = _row_tile(seq, ATTN_TQ)
    tq_c = _row_tile(n_ctx, ATTN_TQ)

    row = lambda v: v.reshape(1, -1)
    kcol = CB_KV * COL_BLOCK // HEAD_DIM
    vcol = kcol + N_KV_HEADS

    for l in range(depth):
        last = l == depth - 1
        mx = [mod[l, 0:1, i * d:(i + 1) * d] for i in range(6)]
        mc = [mod[l, 1:2, i * d:(i + 1) * d] for i in range(6)]
        w_in_l = w_in[l].astype(BF16)
        qg, kg = row(q_gain[l]), row(k_gain[l])
        sg, sb = row(sgu_ln_g[l]), row(sgu_ln_b[l])
        wsp2 = w_spatial[l].astype(BF16).reshape(SGU_GROUPS // 2, 2 * SGU_CHUNK, SGU_CHUNK)
        bsp = jnp.repeat(b_spatial[l].T, SGU_WIDTH // SGU_GROUPS, axis=1)
        mixer_w = (w_br_attn[l].astype(BF16), w_br_sgu[l].astype(BF16), w_br_fourier[l].astype(BF16),
                   w_out[l].astype(BF16), wsp2, bsp)
        w_up_l, w_down_l = w_up[l].astype(BF16), w_down[l].astype(BF16)
        l1g, l1b, l2g, l2b = row(ln1_g[l]), row(ln1_b[l]), row(ln2_g[l]), row(ln2_b[l])

        if last:
            zc = _in_proj(xc, mc[0], mc[1], w_in_l, qg, kg, sg, sb, None, cb0=CB_KV, ncb=1, tm=tm_c)
            ctx_kv = (zc, 0, N_KV_HEADS)
        else:
            zc = _in_proj(xc, mc[0], mc[1], w_in_l, qg, kg, sg, sb, None, cb0=0, ncb=N_COL_BLOCKS, tm=tm_c)
            ctx_kv = (zc, kcol, vcol)
            ac = _attention(zc, ctx_kv, None, tq=tq_c)
            fc = _fourier(zc, tabs_c)
            xc1 = _merge(ac, zc, fc, xc, *mixer_w, mc[2], l1g, l1b, tm=tm_c)
            xc = _ffn(xc1, mc[3], mc[4], mc[5], w_up_l, w_down_l, l2g, l2b, tm=tm_c)

        zx = _in_proj(xl, mx[0], mx[1], w_in_l, qg, kg, sg, sb, rope, cb0=0, ncb=N_COL_BLOCKS, tm=tm_x)
        ax = _attention(zx, ctx_kv, (zx, kcol, vcol), tq=tq_x)
        fx = _fourier(zx, tabs_x)
        xl1 = _merge(ax, zx, fx, xl, *mixer_w, mx[2], l1g, l1b, tm=tm_x)
        xl = _ffn(xl1, mx[3], mx[4], mx[5], w_up_l, w_down_l, l2g, l2b, tm=tm_x)

    return xl[None]
```

```python
import functools
import math

import jax
import jax.numpy as jnp
import numpy as np
from jax import lax
from jax.experimental import pallas as pl
from jax.experimental.pallas import tpu as pltpu

F32 = jnp.float32
BF16 = jnp.bfloat16

D_MODEL = 1024
DEPTH = 2
GRID_W = 64
N_Q_HEADS = 8
N_KV_HEADS = 2
HEAD_DIM = 128
Q_PER_KV = N_Q_HEADS // N_KV_HEADS
ATTN_WIDTH = N_Q_HEADS * HEAD_DIM
KV_WIDTH = N_KV_HEADS * HEAD_DIM
ROPE_AXIS_DIM = HEAD_DIM // 2
ROPE_THETA = 10000.0
SGU_GROUPS = 8
SGU_WIDTH = 512
SGU_CHUNK = 128
FOURIER_GROUPS = 4
FOURIER_WIDTH = 512
FOURIER_GROUP_CH = FOURIER_WIDTH // FOURIER_GROUPS
N_BRANCHES = 3
FFN_HIDDEN = -(-8 * D_MODEL // (3 * 256)) * 256
IN_WIDTH = ATTN_WIDTH + 2 * KV_WIDTH + 2 * SGU_WIDTH + FOURIER_WIDTH + N_BRANCHES * D_MODEL
DEEPNORM_ALPHA = (2 * DEPTH) ** 0.25
LN_EPS = 1e-6
RMS_EPS = 1e-6
ATTN_SCALE = HEAD_DIM ** -0.5
Q_PRESCALE = ATTN_SCALE * math.log2(math.e)

COL_BLOCK = 512
N_COL_BLOCKS = IN_WIDTH // COL_BLOCK
CB_KV = 2
CB_SGU_U = 3
CB_SGU_V = 4
CB_FOURIER = 5
CB_GATES = 6

LANES = 128
V7X_VMEM_BYTES = 64 * 1024 * 1024
VMEM_LIMIT_BYTES = V7X_VMEM_BYTES - 12 * 1024 * 1024

ADA_TN = 1024
ROW_TILE = 512
CTX_ROW_TILE = 256
ATTN_TQ = 256
ATTN_BK = 512
ATTN_UNROLL = 8
FFN_CHUNKS = 2
DFT_POINTS = 128
DFT_A_T2_PER_STEP = 8
DFT_B_K1_PER_STEP = 8


def _cparams(sem):
    return pltpu.CompilerParams(dimension_semantics=sem, vmem_limit_bytes=VMEM_LIMIT_BYTES)


def _row_tile(t, want):
    tm = min(t, want)
    assert t % tm == 0
    return tm


def _sigmoid(x):
    return 1.0 / (1.0 + jnp.exp(-x))


def _gelu_tanh(x):
    c = math.sqrt(2.0 / math.pi)
    return 0.5 * x * (1.0 + jnp.tanh(c * (x + 0.044715 * (x * x * x))))


def _ln(x):
    mu = jnp.mean(x, axis=-1, keepdims=True)
    xc = x - mu
    var = jnp.mean(xc * xc, axis=-1, keepdims=True)
    return xc * lax.rsqrt(var + LN_EPS)


def _ada_kernel(c_ref, w_ref, b_ref, o_ref):
    c = c_ref[...]
    s = c * _sigmoid(c)
    w = w_ref[0]
    b = b_ref[0]
    r0 = jnp.sum(w * s[:, 0:1], axis=0, keepdims=True) + b
    r1 = jnp.sum(w * s[:, 1:2], axis=0, keepdims=True) + b
    o_ref[0] = jnp.concatenate([r0, r1], axis=0)


def _ada(c2, w_ada, b_ada):
    depth, d, n = w_ada.shape
    tn = ADA_TN
    return pl.pallas_call(
        _ada_kernel,
        grid=(depth, n // tn),
        in_specs=[
            pl.BlockSpec((d, 2), lambda l, j: (0, 0)),
            pl.BlockSpec((1, d, tn), lambda l, j: (l, 0, j)),
            pl.BlockSpec((1, 1, tn), lambda l, j: (l, 0, j)),
        ],
        out_specs=pl.BlockSpec((1, 2, tn), lambda l, j: (l, 0, j)),
        out_shape=jax.ShapeDtypeStruct((depth, 2, n), F32),
        compiler_params=_cparams(("arbitrary", "arbitrary")),
        name="ada",
    )(c2, w_ada, b_ada.reshape(depth, 1, n))


def _rms_rope(zh, gain, cos, sin_signed, use_rope):
    y = zh * lax.rsqrt(jnp.mean(zh * zh, axis=-1, keepdims=True) + RMS_EPS) * gain
    if not use_rope:
        return y
    lane = lax.broadcasted_iota(jnp.int32, y.shape, 1)
    nxt = pltpu.roll(y, HEAD_DIM - 1, axis=1)
    prv = pltpu.roll(y, 1, axis=1)
    partner = jnp.where((lane & 1) == 0, nxt, prv)
    return y * cos + partner * sin_signed


def _in_kernel(x_ref, sh_ref, sc_ref, w_ref, qg_ref, kg_ref, lg_ref, lb_ref, cos_ref, sin_ref,
               o_ref, *, cb0, ncb, use_rope):
    h = (_ln(x_ref[...]) * (1.0 + sc_ref[...]) + sh_ref[...]).astype(BF16)
    cos = cos_ref[...] if use_rope else None
    sin = sin_ref[...] if use_rope else None

    def qk_heads(z, base, count, gain):
        for hh in range(count):
            sl = slice(hh * HEAD_DIM, (hh + 1) * HEAD_DIM)
            o_ref[:, base + hh * HEAD_DIM:base + (hh + 1) * HEAD_DIM] = _rms_rope(
                z[:, sl], gain, cos, sin, use_rope).astype(BF16)

    for jj in range(ncb):
        cb = cb0 + jj
        base = jj * COL_BLOCK
        cols = slice(base, base + COL_BLOCK)
        z = jnp.dot(h, w_ref[:, cols], preferred_element_type=F32)
        if cb < CB_KV:
            qk_heads(z, base, COL_BLOCK // HEAD_DIM, qg_ref[...] * Q_PRESCALE)
        elif cb == CB_KV:
            qk_heads(z, base, N_KV_HEADS, kg_ref[...])
            o_ref[:, base + KV_WIDTH:base + COL_BLOCK] = z[:, KV_WIDTH:].astype(BF16)
        elif cb == CB_SGU_U:
            o_ref[:, cols] = _gelu_tanh(z).astype(BF16)
        elif cb == CB_SGU_V:
            o_ref[:, cols] = (_ln(_gelu_tanh(z)) * lg_ref[...] + lb_ref[...]).astype(BF16)
        elif cb == CB_FOURIER:
            o_ref[:, cols] = z.astype(BF16)
        else:
            o_ref[:, cols] = _sigmoid(z).astype(BF16)


def _in_proj(x, shift, scale, w_in, q_gain, k_gain, sgu_g, sgu_b, rope, *, cb0, ncb, tm):
    t, d = x.shape
    assert cb0 % ncb == 0
    width = ncb * COL_BLOCK
    use_rope = rope is not None
    if use_rope:
        cos, sin = rope
    else:
        cos = sin = jnp.zeros((8, HEAD_DIM), F32)
    rope_spec = (pl.BlockSpec((tm, HEAD_DIM), lambda i: (i, 0)) if use_rope
                 else pl.BlockSpec((8, HEAD_DIM), lambda i: (0, 0)))
    vec = lambda n: pl.BlockSpec((1, n), lambda i: (0, 0))
    return pl.pallas_call(
        functools.partial(_in_kernel, cb0=cb0, ncb=ncb, use_rope=use_rope),
        grid=(t // tm,),
        in_specs=[
            pl.BlockSpec((tm, d), lambda i: (i, 0)),
            vec(d), vec(d),
            pl.BlockSpec((d, width), lambda i: (0, cb0 // ncb), pipeline_mode=pl.Buffered(1)),
            vec(HEAD_DIM), vec(HEAD_DIM), vec(SGU_WIDTH), vec(SGU_WIDTH),
            rope_spec, rope_spec,
        ],
        out_specs=pl.BlockSpec((tm, width), lambda i: (i, 0)),
        out_shape=jax.ShapeDtypeStruct((t, width), BF16),
        compiler_params=_cparams(("arbitrary",)),
        name="in_proj",
    )(x, shift, scale, w_in, q_gain, k_gain, sgu_g, sgu_b, cos, sin)


def _attn_kernel(*refs, tq, bk, n_main, bkc):
    if n_main:
        q_ref, kc_ref, vc_ref, k_ref, v_ref, o_ref = refs[:6]
        qs_ref, s0_ref, s1_ref, p0_ref, p1_ref, m_ref, l_ref, acc_ref = refs[6:]
    else:
        q_ref, kc_ref, vc_ref, o_ref = refs[:4]
        qs_ref, s0_ref, p0_ref, m_ref, l_ref, acc_ref = refs[4:]
    ncols_c = bkc // LANES

    for g in range(Q_PER_KV):
        qs_ref[g * tq:(g + 1) * tq, :] = q_ref[:, g * HEAD_DIM:(g + 1) * HEAD_DIM]
    m_ref[...] = jnp.full(m_ref.shape, -jnp.inf, F32)
    l_ref[...] = jnp.zeros(l_ref.shape, F32)
    acc_ref[...] = jnp.zeros(acc_ref.shape, F32)

    def scores(kblk):
        return lax.dot_general(qs_ref[...], kblk, (((1,), (1,)), ((), ())),
                               preferred_element_type=F32)

    def softmax_pv(s_ref, p_ref, vblk, ncols):
        cols = [s_ref[:, c * LANES:(c + 1) * LANES] for c in range(ncols)]
        mx = cols[0]
        for c in cols[1:]:
            mx = jnp.maximum(mx, c)
        m_prev = m_ref[...]
        m_new = jnp.maximum(m_prev, jnp.max(mx, axis=-1, keepdims=True))
        ps = [jnp.exp2(c - m_new) for c in cols]
        psum = ps[0]
        for p in ps[1:]:
            psum = psum + p
        for c in range(ncols):
            p_ref[:, c * LANES:(c + 1) * LANES] = ps[c].astype(BF16)
        alpha = jnp.exp2(m_prev - m_new)
        l_ref[...] = alpha * l_ref[...] + psum
        m_ref[...] = m_new
        acc_ref[...] = alpha * acc_ref[...] + jnp.dot(p_ref[:, :ncols * LANES], vblk,
                                                      preferred_element_type=F32)

    def kblock(ref, i):
        return ref[pl.ds(pl.multiple_of(i * bk, bk), bk), :]

    if n_main:
        ncols = bk // LANES
        s_bufs, p_bufs = (s0_ref, s1_ref), (p0_ref, p1_ref)
        unroll = min(ATTN_UNROLL, n_main)
        assert unroll % 2 == 0 and n_main % unroll == 0
        s0_ref[...] = scores(k_ref[0:bk, :])

        def group(j, last):
            for u in range(unroll):
                b = unroll * j + u
                if u + 1 < unroll or not last:
                    s_bufs[(u + 1) % 2][...] = scores(kblock(k_ref, b + 1))
                else:
                    s0_ref[:, :bkc] = scores(kc_ref[...])
                softmax_pv(s_bufs[u % 2], p_bufs[u % 2], kblock(v_ref, b), ncols)

        def body(j, carry):
            group(j, False)
            return carry

        lax.fori_loop(0, n_main // unroll - 1, body, 0)
        group(n_main // unroll - 1, True)
    else:
        s0_ref[...] = scores(kc_ref[...])
    softmax_pv(s0_ref, p0_ref, vc_ref[...], ncols_c)

    l_row = jnp.sum(l_ref[...], axis=-1, keepdims=True)
    o = acc_ref[...] / l_row
    for g in range(Q_PER_KV):
        o_ref[:, g * HEAD_DIM:(g + 1) * HEAD_DIM] = o[g * tq:(g + 1) * tq, :].astype(BF16)


def _attention(zq, ctx_kv, main_kv, *, tq):
    t = zq.shape[0]
    m_rows = Q_PER_KV * tq
    kv_spec = lambda length, col: pl.BlockSpec((length, HEAD_DIM), lambda h, i: (0, col + h))
    zc, kcol_c, vcol_c = ctx_kv
    bkc = zc.shape[0]
    assert bkc % LANES == 0
    args = [zq, zc, zc]
    in_specs = [pl.BlockSpec((tq, Q_PER_KV * HEAD_DIM), lambda h, i: (i, h)),
                kv_spec(bkc, kcol_c), kv_spec(bkc, vcol_c)]
    scratch = [pltpu.VMEM((m_rows, HEAD_DIM), BF16)]
    if main_kv is not None:
        zm, kcol_m, vcol_m = main_kv
        bk = ATTN_BK
        n_main = zm.shape[0] // bk
        assert n_main * bk == zm.shape[0] and bkc <= bk
        args += [zm, zm]
        in_specs += [kv_spec(zm.shape[0], kcol_m), kv_spec(zm.shape[0], vcol_m)]
        scratch += [pltpu.VMEM((m_rows, bk), F32), pltpu.VMEM((m_rows, bk), F32),
                    pltpu.VMEM((m_rows, bk), BF16), pltpu.VMEM((m_rows, bk), BF16)]
    else:
        bk, n_main = bkc, 0
        scratch += [pltpu.VMEM((m_rows, bkc), F32), pltpu.VMEM((m_rows, bkc), BF16)]
    scratch += [pltpu.VMEM((m_rows, LANES), F32), pltpu.VMEM((m_rows, LANES), F32),
                pltpu.VMEM((m_rows, HEAD_DIM), F32)]
    return pl.pallas_call(
        functools.partial(_attn_kernel, tq=tq, bk=bk, n_main=n_main, bkc=bkc),
        grid=(N_KV_HEADS, t // tq),
        in_specs=in_specs,
        out_specs=pl.BlockSpec((tq, Q_PER_KV * HEAD_DIM), lambda h, i: (i, h)),
        out_shape=jax.ShapeDtypeStruct((t, ATTN_WIDTH), BF16),
        scratch_shapes=scratch,
        compiler_params=_cparams(("arbitrary", "arbitrary")),
        name="attn",
    )(*args)


def _dft_tables(t):
    n2 = DFT_POINTS if t > 2 * DFT_POINTS else t
    n1 = t // n2
    assert n1 * n2 == t

    def cs(n):
        idx = np.arange(n)
        ang = 2.0 * np.pi * ((idx[:, None] * idx[None, :]) % n) / n
        return np.cos(ang), np.sin(ang)

    tabs = {"n1": n1, "n2": n2}
    c2, s2 = cs(n2)
    if n1 > 1:
        c1, s1 = cs(n1)
        tabs["m1"] = jnp.asarray(np.concatenate([c1, -s1], axis=0), F32)
        ang = 2.0 * np.pi * (np.arange(n1)[:, None] * np.arange(n2)[None, :]) / t
        tabs["twr"] = jnp.asarray(np.repeat(np.cos(ang), LANES, axis=1), F32)
        tabs["twi"] = jnp.asarray(np.repeat(-np.sin(ang), LANES, axis=1), F32)
        m2 = np.block([[c2, s2], [-s2, c2]])
    else:
        m2 = np.concatenate([c2, -s2], axis=0)
    tabs["m2"] = jnp.asarray(m2, F32)
    cc, sc = cs(FOURIER_GROUP_CH)
    tabs["mc"] = jnp.asarray(np.concatenate([cc, sc], axis=0), F32)
    tabs["norm"] = 1.0 / math.sqrt(t * FOURIER_GROUP_CH)
    return tabs


def _dft_a_kernel(f_ref, m1_ref, twr_ref, twi_ref, xr_ref, xi_ref, *, n1, nt2):
    w = FOURIER_WIDTH
    for j in range(nt2):
        x1 = jnp.dot(m1_ref[...], f_ref[:, j * w:(j + 1) * w], preferred_element_type=F32)
        wr = twr_ref[:, j * LANES:(j + 1) * LANES]
        wi = twi_ref[:, j * LANES:(j + 1) * LANES]
        for g in range(w // LANES):
            a = x1[:n1, g * LANES:(g + 1) * LANES]
            b = x1[n1:, g * LANES:(g + 1) * LANES]
            sl = slice(j * w + g * LANES, j * w + (g + 1) * LANES)
            xr_ref[:, sl] = (a * wr - b * wi).astype(BF16)
            xi_ref[:, sl] = (a * wi + b * wr).astype(BF16)


def _dft_b_kernel(*refs, n2, kb, complex_in, norm):
    if complex_in:
        xr_ref, xi_ref, m2_ref, mc_ref, o_ref = refs
    else:
        xr_ref, m2_ref, mc_ref, o_ref = refs
    c = FOURIER_GROUP_CH
    ys = []
    for j in range(kb):
        if complex_in:
            rhs = jnp.concatenate([xr_ref[j], xi_ref[j]], axis=0)
        else:
            rhs = xr_ref[j]
        p = jnp.dot(m2_ref[...], rhs, preferred_element_type=F32)
        yg = []
        for g in range(FOURIER_GROUPS):
            pg = jnp.concatenate([p[:n2, g * c:(g + 1) * c], p[n2:, g * c:(g + 1) * c]], axis=1)
            yg.append(jnp.dot(pg.astype(BF16), mc_ref[...], preferred_element_type=F32) * norm)
        ys.append(jnp.concatenate(yg, axis=1))
    if kb == 1:
        o_ref[:, 0, :] = ys[0]
    else:
        o_ref[...] = pltpu.einshape("knw->nkw", jnp.stack(ys, axis=0))


def _fourier(z, tabs):
    t = z.shape[0]
    n1, n2, w = tabs["n1"], tabs["n2"], FOURIER_WIDTH
    if n1 > 1:
        f = z[:, CB_FOURIER * COL_BLOCK:(CB_FOURIER + 1) * COL_BLOCK].reshape(n1, n2 * w)
        nt2 = DFT_A_T2_PER_STEP
        assert n2 % nt2 == 0
        xr, xi = pl.pallas_call(
            functools.partial(_dft_a_kernel, n1=n1, nt2=nt2),
            grid=(n2 // nt2,),
            in_specs=[
                pl.BlockSpec((n1, nt2 * w), lambda i: (0, i)),
                pl.BlockSpec((2 * n1, n1), lambda i: (0, 0)),
                pl.BlockSpec((n1, nt2 * LANES), lambda i: (0, i)),
                pl.BlockSpec((n1, nt2 * LANES), lambda i: (0, i)),
            ],
            out_specs=[pl.BlockSpec((n1, nt2 * w), lambda i: (0, i))] * 2,
            out_shape=[jax.ShapeDtypeStruct((n1, n2 * w), BF16)] * 2,
            compiler_params=_cparams(("arbitrary",)),
            name="dft_a",
        )(f, tabs["m1"].astype(BF16), tabs["twr"], tabs["twi"])
        kb = DFT_B_K1_PER_STEP
        assert n1 % kb == 0
        xs = [xr.reshape(n1, n2, w), xi.reshape(n1, n2, w)]
        x_specs = [pl.BlockSpec((kb, n2, w), lambda i: (i, 0, 0))] * 2
        complex_in = True
    else:
        kb = 1
        xs = [z.reshape(1, t, z.shape[1])]
        x_specs = [pl.BlockSpec((1, n2, w), lambda i: (0, 0, CB_FOURIER))]
        complex_in = False
    m2, mc = tabs["m2"].astype(BF16), tabs["mc"].astype(BF16)
    out = pl.pallas_call(
        functools.partial(_dft_b_kernel, n2=n2, kb=kb, complex_in=complex_in, norm=tabs["norm"]),
        grid=(n1 // kb,),
        in_specs=x_specs + [
            pl.BlockSpec(m2.shape, lambda i: (0, 0)),
            pl.BlockSpec(mc.shape, lambda i: (0, 0)),
        ],
        out_specs=pl.BlockSpec((n2, kb, w), lambda i: (0, i, 0)),
        out_shape=jax.ShapeDtypeStruct((n2, n1, w), F32),
        compiler_params=_cparams(("arbitrary",)),
        name="dft_b",
    )(*xs, m2, mc)
    return out.reshape(t, w)


def _merge_kernel(a_ref, u_ref, vn_ref, fo_ref, g_ref, x_ref, wa_ref, ws_ref, wf_ref, wo_ref,
                  wsp_ref, bsp_ref, gate_ref, lg_ref, lb_ref, o_ref, gout_ref, *, tm):
    group_ch = SGU_WIDTH // SGU_GROUPS
    lane = lax.broadcasted_iota(jnp.int32, (SGU_CHUNK, LANES), 1)
    for ci in range(tm // SGU_CHUNK):
        rows = slice(ci * SGU_CHUNK, (ci + 1) * SGU_CHUNK)
        for lb in range(SGU_WIDTH // LANES):
            cols = slice(lb * LANES, (lb + 1) * LANES)
            y = jnp.dot(wsp_ref[lb], vn_ref[rows, cols], preferred_element_type=F32)
            mixed = jnp.where(lane < group_ch, y[:SGU_CHUNK], y[SGU_CHUNK:]) + bsp_ref[:, cols]
            gout_ref[rows, cols] = (u_ref[rows, cols].astype(F32) * mixed).astype(BF16)

    d = D_MODEL
    merged = g_ref[:, 0:d].astype(F32) * jnp.dot(a_ref[...], wa_ref[...], preferred_element_type=F32)
    merged += g_ref[:, d:2 * d].astype(F32) * jnp.dot(gout_ref[...], ws_ref[...],
                                                      preferred_element_type=F32)
    merged += g_ref[:, 2 * d:3 * d].astype(F32) * jnp.dot(fo_ref[...].astype(BF16), wf_ref[...],
                                                          preferred_element_type=F32)
    mix = jnp.dot(merged.astype(BF16), wo_ref[...], preferred_element_type=F32)
    y = DEEPNORM_ALPHA * x_ref[...] + gate_ref[...] * mix
    o_ref[...] = _ln(y) * lg_ref[...] + lb_ref[...]


def _merge(a, z, f_out, x, w_br_attn, w_br_sgu, w_br_fourier, w_out, wsp2, bsp, gate1, ln_g, ln_b, *, tm):
    t, d = x.shape
    full = lambda arr: pl.BlockSpec(arr.shape, lambda i: (0,) * arr.ndim)
    vec = pl.BlockSpec((1, d), lambda i: (0, 0))
    return pl.pallas_call(
        functools.partial(_merge_kernel, tm=tm),
        grid=(t // tm,),
        in_specs=[
            pl.BlockSpec((tm, ATTN_WIDTH), lambda i: (i, 0)),
            pl.BlockSpec((tm, COL_BLOCK), lambda i: (i, CB_SGU_U)),
            pl.BlockSpec((tm, COL_BLOCK), lambda i: (i, CB_SGU_V)),
            pl.BlockSpec((tm, FOURIER_WIDTH), lambda i: (i, 0)),
            pl.BlockSpec((tm, N_BRANCHES * d), lambda i: (i, CB_GATES * COL_BLOCK // (N_BRANCHES * d))),
            pl.BlockSpec((tm, d), lambda i: (i, 0)),
            full(w_br_attn), full(w_br_sgu), full(w_br_fourier), full(w_out),
            full(wsp2), full(bsp), vec, vec, vec,
        ],
        out_specs=pl.BlockSpec((tm, d), lambda i: (i, 0)),
        out_shape=jax.ShapeDtypeStruct((t, d), F32),
        scratch_shapes=[pltpu.VMEM((tm, SGU_WIDTH), BF16)],
        compiler_params=_cparams(("arbitrary",)),
        name="merge",
    )(a, z, z, f_out, z, x, w_br_attn, w_br_sgu, w_br_fourier, w_out, wsp2, bsp, gate1, ln_g, ln_b)


def _ffn_kernel(x_ref, sh_ref, sc_ref, gate_ref, wup_ref, wd_ref, lg_ref, lb_ref, o_ref, *, hidden):
    th = hidden // FFN_CHUNKS
    half = x_ref.shape[0] // 2
    for r0 in (0, half):
        rows = slice(r0, r0 + half)
        x = x_ref[rows, :]
        h = (_ln(x) * (1.0 + sc_ref[...]) + sh_ref[...]).astype(BF16)
        acc = None
        for c in range(FFN_CHUNKS):
            g = jnp.dot(h, wup_ref[:, c * th:(c + 1) * th], preferred_element_type=F32)
            u = jnp.dot(h, wup_ref[:, hidden + c * th:hidden + (c + 1) * th],
                        preferred_element_type=F32)
            act = (g * _sigmoid(g) * u).astype(BF16)
            part = jnp.dot(act, wd_ref[c * th:(c + 1) * th, :], preferred_element_type=F32)
            acc = part if acc is None else acc + part
        y = DEEPNORM_ALPHA * x + gate_ref[...] * acc
        o_ref[rows, :] = _ln(y) * lg_ref[...] + lb_ref[...]


def _ffn(x, shift, scale, gate, w_up, w_down, ln_g, ln_b, *, tm):
    t, d = x.shape
    hidden = w_down.shape[0]
    th = hidden // FFN_CHUNKS
    assert th * FFN_CHUNKS == hidden and th % LANES == 0
    vec = pl.BlockSpec((1, d), lambda i: (0, 0))
    resident = lambda arr: pl.BlockSpec(arr.shape, lambda i: (0, 0), pipeline_mode=pl.Buffered(1))
    return pl.pallas_call(
        functools.partial(_ffn_kernel, hidden=hidden),
        grid=(t // tm,),
        in_specs=[
            pl.BlockSpec((tm, d), lambda i: (i, 0)),
            vec, vec, vec,
            resident(w_up), resident(w_down),
            vec, vec,
        ],
        out_specs=pl.BlockSpec((tm, d), lambda i: (i, 0)),
        out_shape=jax.ShapeDtypeStruct((t, d), F32),
        compiler_params=_cparams(("arbitrary",)),
        name="ffn",
    )(x, shift, scale, gate, w_up, w_down, ln_g, ln_b)


def _rope_tables(n_tokens):
    rows = n_tokens // GRID_W
    pos_r = np.repeat(np.arange(rows, dtype=np.float64), GRID_W)
    pos_c = np.tile(np.arange(GRID_W, dtype=np.float64), rows)
    inv = ROPE_THETA ** (-np.arange(0, ROPE_AXIS_DIM, 2, dtype=np.float64) / ROPE_AXIS_DIM)
    ang = np.concatenate([pos_r[:, None] * inv, pos_c[:, None] * inv], axis=-1)
    cos, sin = np.cos(ang), np.sin(ang)
    cos2 = np.repeat(cos, 2, axis=-1)
    sin2 = np.stack([-sin, sin], axis=-1).reshape(n_tokens, HEAD_DIM)
    return jnp.asarray(cos2, F32), jnp.asarray(sin2, F32)


def kernel(x, c, ctx, c_ctx, w_ada, b_ada, w_in, q_gain, k_gain, sgu_ln_g, sgu_ln_b, w_spatial,
           b_spatial, w_br_attn, w_br_sgu, w_br_fourier, w_out, ln1_g, ln1_b, w_up, w_down,
           ln2_g, ln2_b):
    batch, seq, d = x.shape
    n_ctx = ctx.shape[1]
    depth = w_in.shape[0]
    assert batch == 1 and d == D_MODEL and depth == DEPTH
    xl = x[0]
    xc = ctx[0]

    rope = _rope_tables(seq)
    tabs_x = _dft_tables(seq)
    tabs_c = _dft_tables(n_ctx)

    c2 = jnp.stack([c[0], c_ctx], axis=1)
    mod = _ada(c2, w_ada, b_ada)

    tm_x = _row_tile(seq, ROW_TILE)
    tm_c = _row_tile(n_ctx, CTX_ROW_TILE)
    tq_x = _row_tile(seq, ATTN_TQ)
    tq_c = _row_tile(n_ctx, ATTN_TQ)

    row = lambda v: v.reshape(1, -1)
    kcol = CB_KV * COL_BLOCK // HEAD_DIM
    vcol = kcol + N_KV_HEADS

    for l in range(depth):
        last = l == depth - 1
        mx = [mod[l, 0:1, i * d:(i + 1) * d] for i in range(6)]
        mc = [mod[l, 1:2, i * d:(i + 1) * d] for i in range(6)]
        w_in_l = w_in[l].astype(BF16)
        qg, kg = row(q_gain[l]), row(k_gain[l])
        sg, sb = row(sgu_ln_g[l]), row(sgu_ln_b[l])
        wsp2 = w_spatial[l].astype(BF16).reshape(SGU_GROUPS // 2, 2 * SGU_CHUNK, SGU_CHUNK)
        bsp = jnp.repeat(b_spatial[l].T, SGU_WIDTH // SGU_GROUPS, axis=1)
        mixer_w = (w_br_attn[l].astype(BF16), w_br_sgu[l].astype(BF16), w_br_fourier[l].astype(BF16),
                   w_out[l].astype(BF16), wsp2, bsp)
        w_up_l, w_down_l = w_up[l].astype(BF16), w_down[l].astype(BF16)
        l1g, l1b, l2g, l2b = row(ln1_g[l]), row(ln1_b[l]), row(ln2_g[l]), row(ln2_b[l])

        if last:
            zc = _in_proj(xc, mc[0], mc[1], w_in_l, qg, kg, sg, sb, None, cb0=CB_KV, ncb=1, tm=tm_c)
            ctx_kv = (zc, 0, N_KV_HEADS)
        else:
            zc = _in_proj(xc, mc[0], mc[1], w_in_l, qg, kg, sg, sb, None, cb0=0, ncb=N_COL_BLOCKS, tm=tm_c)
            ctx_kv = (zc, kcol, vcol)
            ac = _attention(zc, ctx_kv, None, tq=tq_c)
            fc = _fourier(zc, tabs_c)
            xc1 = _merge(ac, zc, fc, xc, *mixer_w, mc[2], l1g, l1b, tm=tm_c)
            xc = _ffn(xc1, mc[3], mc[4], mc[5], w_up_l, w_down_l, l2g, l2b, tm=tm_c)

        zx = _in_proj(xl, mx[0], mx[1], w_in_l, qg, kg, sg, sb, rope, cb0=0, ncb=N_COL_BLOCKS, tm=tm_x)
        ax = _attention(zx, ctx_kv, (zx, kcol, vcol), tq=tq_x)
        fx = _fourier(zx, tabs_x)
        xl1 = _merge(ax, zx, fx, xl, *mixer_w, mx[2], l1g, l1b, tm=tm_x)
        xl = _ffn(xl1, mx[3], mx[4], mx[5], w_up_l, w_down_l, l2g, l2b, tm=tm_x)

    return xl[None]
```

```python
import functools
import math

import jax
import jax.numpy as jnp
import numpy as np
from jax import lax
from jax.experimental import pallas as pl
from jax.experimental.pallas import tpu as pltpu

F32 = jnp.float32
BF16 = jnp.bfloat16

D_MODEL = 1024
DEPTH = 2
GRID_W = 64
N_Q_HEADS = 8
N_KV_HEADS = 2
HEAD_DIM = 128
Q_PER_KV = N_Q_HEADS // N_KV_HEADS
ATTN_WIDTH = N_Q_HEADS * HEAD_DIM
KV_WIDTH = N_KV_HEADS * HEAD_DIM
ROPE_AXIS_DIM = HEAD_DIM // 2
ROPE_THETA = 10000.0
SGU_GROUPS = 8
SGU_WIDTH = 512
SGU_CHUNK = 128
FOURIER_GROUPS = 4
FOURIER_WIDTH = 512
FOURIER_GROUP_CH = FOURIER_WIDTH // FOURIER_GROUPS
N_BRANCHES = 3
FFN_HIDDEN = -(-8 * D_MODEL // (3 * 256)) * 256
IN_WIDTH = ATTN_WIDTH + 2 * KV_WIDTH + 2 * SGU_WIDTH + FOURIER_WIDTH + N_BRANCHES * D_MODEL
DEEPNORM_ALPHA = (2 * DEPTH) ** 0.25
LN_EPS = 1e-6
RMS_EPS = 1e-6
ATTN_SCALE = HEAD_DIM ** -0.5
Q_PRESCALE = ATTN_SCALE * math.log2(math.e)

COL_BLOCK = 512
N_COL_BLOCKS = IN_WIDTH // COL_BLOCK
CB_KV = 2
CB_SGU_U = 3
CB_SGU_V = 4
CB_FOURIER = 5
CB_GATES = 6

LANES = 128
V7X_VMEM_BYTES = 64 * 1024 * 1024
VMEM_LIMIT_BYTES = V7X_VMEM_BYTES - 12 * 1024 * 1024

ADA_TN = 1024
ROW_TILE = 512
CTX_ROW_TILE = 256
ATTN_TQ = 256
ATTN_BK = 512
ATTN_UNROLL = 8
FFN_CHUNKS = 2
DFT_POINTS = 128
DFT_A_T2_PER_STEP = 8
DFT_B_K1_PER_STEP = 8


def _cparams(sem):
    return pltpu.CompilerParams(dimension_semantics=sem, vmem_limit_bytes=VMEM_LIMIT_BYTES)


def _row_tile(t, want):
    tm = min(t, want)
    assert t % tm == 0
    return tm


def _sigmoid(x):
    return 1.0 / (1.0 + jnp.exp(-x))


def _gelu_tanh(x):
    c = math.sqrt(2.0 / math.pi)
    return 0.5 * x * (1.0 + jnp.tanh(c * (x + 0.044715 * (x * x * x))))


def _ln(x):
    mu = jnp.mean(x, axis=-1, keepdims=True)
    xc = x - mu
    var = jnp.mean(xc * xc, axis=-1, keepdims=True)
    return xc * lax.rsqrt(var + LN_EPS)


def _ada_kernel(c_ref, w_ref, b_ref, o_ref):
    c = c_ref[...]
    s = c * _sigmoid(c)
    w = w_ref[0]
    b = b_ref[0]
    r0 = jnp.sum(w * s[:, 0:1], axis=0, keepdims=True) + b
    r1 = jnp.sum(w * s[:, 1:2], axis=0, keepdims=True) + b
    o_ref[0] = jnp.concatenate([r0, r1], axis=0)


def _ada(c2, w_ada, b_ada):
    depth, d, n = w_ada.shape
    tn = ADA_TN
    return pl.pallas_call(
        _ada_kernel,
        grid=(depth, n // tn),
        in_specs=[
            pl.BlockSpec((d, 2), lambda l, j: (0, 0)),
            pl.BlockSpec((1, d, tn), lambda l, j: (l, 0, j)),
            pl.BlockSpec((1, 1, tn), lambda l, j: (l, 0, j)),
        ],
        out_specs=pl.BlockSpec((1, 2, tn), lambda l, j: (l, 0, j)),
        out_shape=jax.ShapeDtypeStruct((depth, 2, n), F32),
        compiler_params=_cparams(("arbitrary", "arbitrary")),
        name="ada",
    )(c2, w_ada, b_ada.reshape(depth, 1, n))


def _rms_rope(zh, gain, cos, sin_signed, use_rope):
    y = zh * lax.rsqrt(jnp.mean(zh * zh, axis=-1, keepdims=True) + RMS_EPS) * gain
    if not use_rope:
        return y
    lane = lax.broadcasted_iota(jnp.int32, y.shape, 1)
    nxt = pltpu.roll(y, HEAD_DIM - 1, axis=1)
    prv = pltpu.roll(y, 1, axis=1)
    partner = jnp.where((lane & 1) == 0, nxt, prv)
    return y * cos + partner * sin_signed


def _in_kernel(x_ref, sh_ref, sc_ref, w_ref, qg_ref, kg_ref, lg_ref, lb_ref, cos_ref, sin_ref,
               o_ref, *, cb0, ncb, use_rope):
    h = (_ln(x_ref[...]) * (1.0 + sc_ref[...]) + sh_ref[...]).astype(BF16)
    cos = cos_ref[...] if use_rope else None
    sin = sin_ref[...] if use_rope else None

    def qk_heads(z, base, count, gain):
        for hh in range(count):
            sl = slice(hh * HEAD_DIM, (hh + 1) * HEAD_DIM)
            o_ref[:, base + hh * HEAD_DIM:base + (hh + 1) * HEAD_DIM] = _rms_rope(
                z[:, sl], gain, cos, sin, use_rope).astype(BF16)

    for jj in range(ncb):
        cb = cb0 + jj
        base = jj * COL_BLOCK
        cols = slice(base, base + COL_BLOCK)
        z = jnp.dot(h, w_ref[:, cols], preferred_element_type=F32)
        if cb < CB_KV:
            qk_heads(z, base, COL_BLOCK // HEAD_DIM, qg_ref[...] * Q_PRESCALE)
        elif cb == CB_KV:
            qk_heads(z, base, N_KV_HEADS, kg_ref[...])
            o_ref[:, base + KV_WIDTH:base + COL_BLOCK] = z[:, KV_WIDTH:].astype(BF16)
        elif cb == CB_SGU_U:
            o_ref[:, cols] = _gelu_tanh(z).astype(BF16)
        elif cb == CB_SGU_V:
            o_ref[:, cols] = (_ln(_gelu_tanh(z)) * lg_ref[...] + lb_ref[...]).astype(BF16)
        elif cb == CB_FOURIER:
            o_ref[:, cols] = z.astype(BF16)
        else:
            o_ref[:, cols] = _sigmoid(z).astype(BF16)


def _in_proj(x, shift, scale, w_in, q_gain, k_gain, sgu_g, sgu_b, rope, *, cb0, ncb, tm):
    t, d = x.shape
    assert cb0 % ncb == 0
    width = ncb * COL_BLOCK
    use_rope = rope is not None
    if use_rope:
        cos, sin = rope
    else:
        cos = sin = jnp.zeros((8, HEAD_DIM), F32)
    rope_spec = (pl.BlockSpec((tm, HEAD_DIM), lambda i: (i, 0)) if use_rope
                 else pl.BlockSpec((8, HEAD_DIM), lambda i: (0, 0)))
    vec = lambda n: pl.BlockSpec((1, n), lambda i: (0, 0))
    return pl.pallas_call(
        functools.partial(_in_kernel, cb0=cb0, ncb=ncb, use_rope=use_rope),
        grid=(t // tm,),
        in_specs=[
            pl.BlockSpec((tm, d), lambda i: (i, 0)),
            vec(d), vec(d),
            pl.BlockSpec((d, width), lambda i: (0, cb0 // ncb), pipeline_mode=pl.Buffered(1)),
            vec(HEAD_DIM), vec(HEAD_DIM), vec(SGU_WIDTH), vec(SGU_WIDTH),
            rope_spec, rope_spec,
        ],
        out_specs=pl.BlockSpec((tm, width), lambda i: (i, 0)),
        out_shape=jax.ShapeDtypeStruct((t, width), BF16),
        compiler_params=_cparams(("arbitrary",)),
        name="in_proj",
    )(x, shift, scale, w_in, q_gain, k_gain, sgu_g, sgu_b, cos, sin)


def _attn_kernel(*refs, tq, bk, n_main, bkc):
    if n_main:
        q_ref, kc_ref, vc_ref, k_ref, v_ref, o_ref = refs[:6]
        qs_ref, s0_ref, s1_ref, p0_ref, p1_ref, m_ref, l_ref, acc_ref = refs[6:]
    else:
        q_ref, kc_ref, vc_ref, o_ref = refs[:4]
        qs_ref, s0_ref, p0_ref, m_ref, l_ref, acc_ref = refs[4:]
    ncols_c = bkc // LANES

    for g in range(Q_PER_KV):
        qs_ref[g * tq:(g + 1) * tq, :] = q_ref[:, g * HEAD_DIM:(g + 1) * HEAD_DIM]
    m_ref[...] = jnp.full(m_ref.shape, -jnp.inf, F32)
    l_ref[...] = jnp.zeros(l_ref.shape, F32)
    acc_ref[...] = jnp.zeros(acc_ref.shape, F32)

    def scores(kblk):
        return lax.dot_general(qs_ref[...], kblk, (((1,), (1,)), ((), ())),
                               preferred_element_type=F32)

    def softmax_pv(s_ref, p_ref, vblk, ncols):
        cols = [s_ref[:, c * LANES:(c + 1) * LANES] for c in range(ncols)]
        mx = cols[0]
        for c in cols[1:]:
            mx = jnp.maximum(mx, c)
        m_prev = m_ref[...]
        m_new = jnp.maximum(m_prev, jnp.max(mx, axis=-1, keepdims=True))
        ps = [jnp.exp2(c - m_new) for c in cols]
        psum = ps[0]
        for p in ps[1:]:
            psum = psum + p
        for c in range(ncols):
            p_ref[:, c * LANES:(c + 1) * LANES] = ps[c].astype(BF16)
        alpha = jnp.exp2(m_prev - m_new)
        l_ref[...] = alpha * l_ref[...] + psum
        m_ref[...] = m_new
        acc_ref[...] = alpha * acc_ref[...] + jnp.dot(p_ref[:, :ncols * LANES], vblk,
                                                      preferred_element_type=F32)

    def kblock(ref, i):
        return ref[pl.ds(pl.multiple_of(i * bk, bk), bk), :]

    if n_main:
        ncols = bk // LANES
        s_bufs, p_bufs = (s0_ref, s1_ref), (p0_ref, p1_ref)
        unroll = min(ATTN_UNROLL, n_main)
        assert unroll % 2 == 0 and n_main % unroll == 0
        s0_ref[...] = scores(k_ref[0:bk, :])

        def group(j, last):
            for u in range(unroll):
                b = unroll * j + u
                if u + 1 < unroll or not last:
                    s_bufs[(u + 1) % 2][...] = scores(kblock(k_ref, b + 1))
                else:
                    s0_ref[:, :bkc] = scores(kc_ref[...])
                softmax_pv(s_bufs[u % 2], p_bufs[u % 2], kblock(v_ref, b), ncols)

        def body(j, carry):
            group(j, False)
            return carry

        lax.fori_loop(0, n_main // unroll - 1, body, 0)
        group(n_main // unroll - 1, True)
    else:
        s0_ref[...] = scores(kc_ref[...])
    softmax_pv(s0_ref, p0_ref, vc_ref[...], ncols_c)

    l_row = jnp.sum(l_ref[...], axis=-1, keepdims=True)
    o = acc_ref[...] / l_row
    for g in range(Q_PER_KV):
        o_ref[:, g * HEAD_DIM:(g + 1) * HEAD_DIM] = o[g * tq:(g + 1) * tq, :].astype(BF16)


def _attention(zq, ctx_kv, main_kv, *, tq):
    t = zq.shape[0]
    m_rows = Q_PER_KV * tq
    kv_spec = lambda length, col: pl.BlockSpec((length, HEAD_DIM), lambda h, i: (0, col + h))
    zc, kcol_c, vcol_c = ctx_kv
    bkc = zc.shape[0]
    assert bkc % LANES == 0
    args = [zq, zc, zc]
    in_specs = [pl.BlockSpec((tq, Q_PER_KV * HEAD_DIM), lambda h, i: (i, h)),
                kv_spec(bkc, kcol_c), kv_spec(bkc, vcol_c)]
    scratch = [pltpu.VMEM((m_rows, HEAD_DIM), BF16)]
    if main_kv is not None:
        zm, kcol_m, vcol_m = main_kv
        bk = ATTN_BK
        n_main = zm.shape[0] // bk
        assert n_main * bk == zm.shape[0] and bkc <= bk
        args += [zm, zm]
        in_specs += [kv_spec(zm.shape[0], kcol_m), kv_spec(zm.shape[0], vcol_m)]
        scratch += [pltpu.VMEM((m_rows, bk), F32), pltpu.VMEM((m_rows, bk), F32),
                    pltpu.VMEM((m_rows, bk), BF16), pltpu.VMEM((m_rows, bk), BF16)]
    else:
        bk, n_main = bkc, 0
        scratch += [pltpu.VMEM((m_rows, bkc), F32), pltpu.VMEM((m_rows, bkc), BF16)]
    scratch += [pltpu.VMEM((m_rows, LANES), F32), pltpu.VMEM((m_rows, LANES), F32),
                pltpu.VMEM((m_rows, HEAD_DIM), F32)]
    return pl.pallas_call(
        functools.partial(_attn_kernel, tq=tq, bk=bk, n_main=n_main, bkc=bkc),
        grid=(N_KV_HEADS, t // tq),
        in_specs=in_specs,
        out_specs=pl.BlockSpec((tq, Q_PER_KV * HEAD_DIM), lambda h, i: (i, h)),
        out_shape=jax.ShapeDtypeStruct((t, ATTN_WIDTH), BF16),
        scratch_shapes=scratch,
        compiler_params=_cparams(("arbitrary", "arbitrary")),
        name="attn",
    )(*args)


def _dft_tables(t):
    n2 = DFT_POINTS if t > 2 * DFT_POINTS else t
    n1 = t // n2
    assert n1 * n2 == t

    def cs(n):
        idx = np.arange(n)
        ang = 2.0 * np.pi * ((idx[:, None] * idx[None, :]) % n) / n
        return np.cos(ang), np.sin(ang)

    tabs = {"n1": n1, "n2": n2}
    c2, s2 = cs(n2)
    if n1 > 1:
        c1, s1 = cs(n1)
        tabs["m1"] = jnp.asarray(np.concatenate([c1, -s1], axis=0), F32)
        ang = 2.0 * np.pi * (np.arange(n1)[:, None] * np.arange(n2)[None, :]) / t
        tabs["twr"] = jnp.asarray(np.repeat(np.cos(ang), LANES, axis=1), F32)
        tabs["twi"] = jnp.asarray(np.repeat(-np.sin(ang), LANES, axis=1), F32)
        m2 = np.block([[c2, s2], [-s2, c2]])
    else:
        m2 = np.concatenate([c2, -s2], axis=0)
    tabs["m2"] = jnp.asarray(m2, F32)
    cc, sc = cs(FOURIER_GROUP_CH)
    tabs["mc"] = jnp.asarray(np.concatenate([cc, sc], axis=0), F32)
    tabs["norm"] = 1.0 / math.sqrt(t * FOURIER_GROUP_CH)
    return tabs


def _dft_a_kernel(f_ref, m1_ref, twr_ref, twi_ref, xr_ref, xi_ref, *, n1, nt2):
    w = FOURIER_WIDTH
    for j in range(nt2):
        x1 = jnp.dot(m1_ref[...], f_ref[:, j * w:(j + 1) * w], preferred_element_type=F32)
        wr = twr_ref[:, j * LANES:(j + 1) * LANES]
        wi = twi_ref[:, j * LANES:(j + 1) * LANES]
        for g in range(w // LANES):
            a = x1[:n1, g * LANES:(g + 1) * LANES]
            b = x1[n1:, g * LANES:(g + 1) * LANES]
            sl = slice(j * w + g * LANES, j * w + (g + 1) * LANES)
            xr_ref[:, sl] = (a * wr - b * wi).astype(BF16)
            xi_ref[:, sl] = (a * wi + b * wr).astype(BF16)


def _dft_b_kernel(*refs, n2, kb, complex_in, norm):
    if complex_in:
        xr_ref, xi_ref, m2_ref, mc_ref, o_ref = refs
    else:
        xr_ref, m2_ref, mc_ref, o_ref = refs
    c = FOURIER_GROUP_CH
    ys = []
    for j in range(kb):
        if complex_in:
            rhs = jnp.concatenate([xr_ref[j], xi_ref[j]], axis=0)
        else:
            rhs = xr_ref[j]
        p = jnp.dot(m2_ref[...], rhs, preferred_element_type=F32)
        yg = []
        for g in range(FOURIER_GROUPS):
            pg = jnp.concatenate([p[:n2, g * c:(g + 1) * c], p[n2:, g * c:(g + 1) * c]], axis=1)
            yg.append(jnp.dot(pg.astype(BF16), mc_ref[...], preferred_element_type=F32) * norm)
        ys.append(jnp.concatenate(yg, axis=1))
    if kb == 1:
        o_ref[:, 0, :] = ys[0]
    else:
        o_ref[...] = jnp.transpose(jnp.stack(ys, axis=0), (1, 0, 2))


def _fourier(z, tabs):
    t = z.shape[0]
    n1, n2, w = tabs["n1"], tabs["n2"], FOURIER_WIDTH
    if n1 > 1:
        f = z[:, CB_FOURIER * COL_BLOCK:(CB_FOURIER + 1) * COL_BLOCK].reshape(n1, n2 * w)
        nt2 = DFT_A_T2_PER_STEP
        assert n2 % nt2 == 0
        xr, xi = pl.pallas_call(
            functools.partial(_dft_a_kernel, n1=n1, nt2=nt2),
            grid=(n2 // nt2,),
            in_specs=[
                pl.BlockSpec((n1, nt2 * w), lambda i: (0, i)),
                pl.BlockSpec((2 * n1, n1), lambda i: (0, 0)),
                pl.BlockSpec((n1, nt2 * LANES), lambda i: (0, i)),
                pl.BlockSpec((n1, nt2 * LANES), lambda i: (0, i)),
            ],
            out_specs=[pl.BlockSpec((n1, nt2 * w), lambda i: (0, i))] * 2,
            out_shape=[jax.ShapeDtypeStruct((n1, n2 * w), BF16)] * 2,
            compiler_params=_cparams(("arbitrary",)),
            name="dft_a",
        )(f, tabs["m1"].astype(BF16), tabs["twr"], tabs["twi"])
        kb = DFT_B_K1_PER_STEP
        assert n1 % kb == 0
        xs = [xr.reshape(n1, n2, w), xi.reshape(n1, n2, w)]
        x_specs = [pl.BlockSpec((kb, n2, w), lambda i: (i, 0, 0))] * 2
        complex_in = True
    else:
        kb = 1
        xs = [z.reshape(1, t, z.shape[1])]
        x_specs = [pl.BlockSpec((1, n2, w), lambda i: (0, 0, CB_FOURIER))]
        complex_in = False
    m2, mc = tabs["m2"].astype(BF16), tabs["mc"].astype(BF16)
    out = pl.pallas_call(
        functools.partial(_dft_b_kernel, n2=n2, kb=kb, complex_in=complex_in, norm=tabs["norm"]),
        grid=(n1 // kb,),
        in_specs=x_specs + [
            pl.BlockSpec(m2.shape, lambda i: (0, 0)),
            pl.BlockSpec(mc.shape, lambda i: (0, 0)),
        ],
        out_specs=pl.BlockSpec((n2, kb, w), lambda i: (0, i, 0)),
        out_shape=jax.ShapeDtypeStruct((n2, n1, w), F32),
        compiler_params=_cparams(("arbitrary",)),
        name="dft_b",
    )(*xs, m2, mc)
    return out.reshape(t, w)


def _merge_kernel(a_ref, u_ref, vn_ref, fo_ref, g_ref, x_ref, wa_ref, ws_ref, wf_ref, wo_ref,
                  wsp_ref, bsp_ref, gate_ref, lg_ref, lb_ref, o_ref, gout_ref, *, tm):
    group_ch = SGU_WIDTH // SGU_GROUPS
    lane = lax.broadcasted_iota(jnp.int32, (SGU_CHUNK, LANES), 1)
    for ci in range(tm // SGU_CHUNK):
        rows = slice(ci * SGU_CHUNK, (ci + 1) * SGU_CHUNK)
        for lb in range(SGU_WIDTH // LANES):
            cols = slice(lb * LANES, (lb + 1) * LANES)
            y = jnp.dot(wsp_ref[lb], vn_ref[rows, cols], preferred_element_type=F32)
            mixed = jnp.where(lane < group_ch, y[:SGU_CHUNK], y[SGU_CHUNK:]) + bsp_ref[:, cols]
            gout_ref[rows, cols] = (u_ref[rows, cols].astype(F32) * mixed).astype(BF16)

    d = D_MODEL
    merged = g_ref[:, 0:d].astype(F32) * jnp.dot(a_ref[...], wa_ref[...], preferred_element_type=F32)
    merged += g_ref[:, d:2 * d].astype(F32) * jnp.dot(gout_ref[...], ws_ref[...],
                                                      preferred_element_type=F32)
    merged += g_ref[:, 2 * d:3 * d].astype(F32) * jnp.dot(fo_ref[...].astype(BF16), wf_ref[...],
                                                          preferred_element_type=F32)
    mix = jnp.dot(merged.astype(BF16), wo_ref[...], preferred_element_type=F32)
    y = DEEPNORM_ALPHA * x_ref[...] + gate_ref[...] * mix
    o_ref[...] = _ln(y) * lg_ref[...] + lb_ref[...]


def _merge(a, z, f_out, x, w_br_attn, w_br_sgu, w_br_fourier, w_out, wsp2, bsp, gate1, ln_g, ln_b, *, tm):
    t, d = x.shape
    full = lambda arr: pl.BlockSpec(arr.shape, lambda i: (0,) * arr.ndim)
    vec = pl.BlockSpec((1, d), lambda i: (0, 0))
    return pl.pallas_call(
        functools.partial(_merge_kernel, tm=tm),
        grid=(t // tm,),
        in_specs=[
            pl.BlockSpec((tm, ATTN_WIDTH), lambda i: (i, 0)),
            pl.BlockSpec((tm, COL_BLOCK), lambda i: (i, CB_SGU_U)),
            pl.BlockSpec((tm, COL_BLOCK), lambda i: (i, CB_SGU_V)),
            pl.BlockSpec((tm, FOURIER_WIDTH), lambda i: (i, 0)),
            pl.BlockSpec((tm, N_BRANCHES * d), lambda i: (i, CB_GATES * COL_BLOCK // (N_BRANCHES * d))),
            pl.BlockSpec((tm, d), lambda i: (i, 0)),
            full(w_br_attn), full(w_br_sgu), full(w_br_fourier), full(w_out),
            full(wsp2), full(bsp), vec, vec, vec,
        ],
        out_specs=pl.BlockSpec((tm, d), lambda i: (i, 0)),
        out_shape=jax.ShapeDtypeStruct((t, d), F32),
        scratch_shapes=[pltpu.VMEM((tm, SGU_WIDTH), BF16)],
        compiler_params=_cparams(("arbitrary",)),
        name="merge",
    )(a, z, z, f_out, z, x, w_br_attn, w_br_sgu, w_br_fourier, w_out, wsp2, bsp, gate1, ln_g, ln_b)


def _ffn_kernel(x_ref, sh_ref, sc_ref, gate_ref, wup_ref, wd_ref, lg_ref, lb_ref, o_ref, *, hidden):
    th = hidden // FFN_CHUNKS
    half = x_ref.shape[0] // 2
    for r0 in (0, half):
        rows = slice(r0, r0 + half)
        x = x_ref[rows, :]
        h = (_ln(x) * (1.0 + sc_ref[...]) + sh_ref[...]).astype(BF16)
        acc = None
        for c in range(FFN_CHUNKS):
            g = jnp.dot(h, wup_ref[:, c * th:(c + 1) * th], preferred_element_type=F32)
            u = jnp.dot(h, wup_ref[:, hidden + c * th:hidden + (c + 1) * th],
                        preferred_element_type=F32)
            act = (g * _sigmoid(g) * u).astype(BF16)
            part = jnp.dot(act, wd_ref[c * th:(c + 1) * th, :], preferred_element_type=F32)
            acc = part if acc is None else acc + part
        y = DEEPNORM_ALPHA * x + gate_ref[...] * acc
        o_ref[rows, :] = _ln(y) * lg_ref[...] + lb_ref[...]


def _ffn(x, shift, scale, gate, w_up, w_down, ln_g, ln_b, *, tm):
    t, d = x.shape
    hidden = w_down.shape[0]
    th = hidden // FFN_CHUNKS
    assert th * FFN_CHUNKS == hidden and th % LANES == 0
    vec = pl.BlockSpec((1, d), lambda i: (0, 0))
    resident = lambda arr: pl.BlockSpec(arr.shape, lambda i: (0, 0), pipeline_mode=pl.Buffered(1))
    return pl.pallas_call(
        functools.partial(_ffn_kernel, hidden=hidden),
        grid=(t // tm,),
        in_specs=[
            pl.BlockSpec((tm, d), lambda i: (i, 0)),
            vec, vec, vec,
            resident(w_up), resident(w_down),
            vec, vec,
        ],
        out_specs=pl.BlockSpec((tm, d), lambda i: (i, 0)),
        out_shape=jax.ShapeDtypeStruct((t, d), F32),
        compiler_params=_cparams(("arbitrary",)),
        name="ffn",
    )(x, shift, scale, gate, w_up, w_down, ln_g, ln_b)


def _rope_tables(n_tokens):
    rows = n_tokens // GRID_W
    pos_r = np.repeat(np.arange(rows, dtype=np.float64), GRID_W)
    pos_c = np.tile(np.arange(GRID_W, dtype=np.float64), rows)
    inv = ROPE_THETA ** (-np.arange(0, ROPE_AXIS_DIM, 2, dtype=np.float64) / ROPE_AXIS_DIM)
    ang = np.concatenate([pos_r[:, None] * inv, pos_c[:, None] * inv], axis=-1)
    cos, sin = np.cos(ang), np.sin(ang)
    cos2 = np.repeat(cos, 2, axis=-1)
    sin2 = np.stack([-sin, sin], axis=-1).reshape(n_tokens, HEAD_DIM)
    return jnp.asarray(cos2, F32), jnp.asarray(sin2, F32)


def kernel(x, c, ctx, c_ctx, w_ada, b_ada, w_in, q_gain, k_gain, sgu_ln_g, sgu_ln_b, w_spatial,
           b_spatial, w_br_attn, w_br_sgu, w_br_fourier, w_out, ln1_g, ln1_b, w_up, w_down,
           ln2_g, ln2_b):
    batch, seq, d = x.shape
    n_ctx = ctx.shape[1]
    depth = w_in.shape[0]
    assert batch == 1 and d == D_MODEL and depth == DEPTH
    xl = x[0]
    xc = ctx[0]

    rope = _rope_tables(seq)
    tabs_x = _dft_tables(seq)
    tabs_c = _dft_tables(n_ctx)

    c2 = jnp.stack([c[0], c_ctx], axis=1)
    mod = _ada(c2, w_ada, b_ada)

    tm_x = _row_tile(seq, ROW_TILE)
    tm_c = _row_tile(n_ctx, CTX_ROW_TILE)
    tq_x = _row_tile(seq, ATTN_TQ)
    tq_c = _row_tile(n_ctx, ATTN_TQ)

    row = lambda v: v.reshape(1, -1)
    kcol = CB_KV * COL_BLOCK // HEAD_DIM
    vcol = kcol + N_KV_HEADS

    for l in range(depth):
        last = l == depth - 1
        mx = [mod[l, 0:1, i * d:(i + 1) * d] for i in range(6)]
        mc = [mod[l, 1:2, i * d:(i + 1) * d] for i in range(6)]
        w_in_l = w_in[l].astype(BF16)
        qg, kg = row(q_gain[l]), row(k_gain[l])
        sg, sb = row(sgu_ln_g[l]), row(sgu_ln_b[l])
        wsp2 = w_spatial[l].astype(BF16).reshape(SGU_GROUPS // 2, 2 * SGU_CHUNK, SGU_CHUNK)
        bsp = jnp.repeat(b_spatial[l].T, SGU_WIDTH // SGU_GROUPS, axis=1)
        mixer_w = (w_br_attn[l].astype(BF16), w_br_sgu[l].astype(BF16), w_br_fourier[l].astype(BF16),
                   w_out[l].astype(BF16), wsp2, bsp)
        w_up_l, w_down_l = w_up[l].astype(BF16), w_down[l].astype(BF16)
        l1g, l1b, l2g, l2b = row(ln1_g[l]), row(ln1_b[l]), row(ln2_g[l]), row(ln2_b[l])

        if last:
            zc = _in_proj(xc, mc[0], mc[1], w_in_l, qg, kg, sg, sb, None, cb0=CB_KV, ncb=1, tm=tm_c)
            ctx_kv = (zc, 0, N_KV_HEADS)
        else:
            zc = _in_proj(xc, mc[0], mc[1], w_in_l, qg, kg, sg, sb, None, cb0=0, ncb=N_COL_BLOCKS, tm=tm_c)
            ctx_kv = (zc, kcol, vcol)
            ac = _attention(zc, ctx_kv, None, tq=tq_c)
            fc = _fourier(zc, tabs_c)
            xc1 = _merge(ac, zc, fc, xc, *mixer_w, mc[2], l1g, l1b, tm=tm_c)
            xc = _ffn(xc1, mc[3], mc[4], mc[5], w_up_l, w_down_l, l2g, l2b, tm=tm_c)

        zx = _in_proj(xl, mx[0], mx[1], w_in_l, qg, kg, sg, sb, rope, cb0=0, ncb=N_COL_BLOCKS, tm=tm_x)
        ax = _attention(zx, ctx_kv, (zx, kcol, vcol), tq=tq_x)
        fx = _fourier(zx, tabs_x)
        xl1 = _merge(ax, zx, fx, xl, *mixer_w, mx[2], l1g, l1b, tm=tm_x)
        xl = _ffn(xl1, mx[3], mx[4], mx[5], w_up_l, w_down_l, l2g, l2b, tm=tm_x)

    return xl[None]
```
